```python
import jax, jax.numpy as jnp
from jax import lax
import numpy as np

D_MODEL = 1024
BATCH = 32
SEQ = 2048
DEPTH = 1

CTX_LEN = 256
GRID_W = 64
NORM_EPS = 1e-6
SGU_WIDTH = 1024
SGU_GROUPS = 8
SGU_GROUP_DIM = SGU_WIDTH // SGU_GROUPS
CHUNK = 128
TINY_DIM = 64
ATTN_BLOCK = 128
ROPE_BASE = 10000.0
NA_HEADS = 16
NA_HEAD_DIM = 32
NA_WIDTH = NA_HEADS * NA_HEAD_DIM
NA_ROWS = 8
NA_COLS = 16
N_EXPERTS = 64
TOP_K = 8
EXPERT_FF = 256
SHARED_FF = 256
ROUTE_SCALE = 2.5
EXPERT_BLOCK = 128
COL_U = 0
COL_V = COL_U + SGU_WIDTH
COL_TQ = COL_V + SGU_WIDTH
COL_TK = COL_TQ + TINY_DIM
COL_TV = COL_TK + TINY_DIM
COL_NQ = COL_TV + TINY_DIM
COL_NK = COL_NQ + NA_WIDTH
COL_NV = COL_NK + NA_WIDTH
COL_GA = COL_NV + NA_WIDTH
COL_GB = COL_GA + D_MODEL
IN_WIDTH = COL_GB + D_MODEL

kernel_name = "hybrid_sgu_natten_moe_dit_block"


def rms_norm(x, g):
    xf = x.astype(jnp.float32)
    y = xf * lax.rsqrt(jnp.mean(xf * xf, -1, keepdims=True) + NORM_EPS)
    return (y * g.astype(jnp.float32)).astype(x.dtype)


def layer_norm(x, g, b):
    xf = x.astype(jnp.float32)
    mu = jnp.mean(xf, -1, keepdims=True)
    var = jnp.mean(jnp.square(xf - mu), -1, keepdims=True)
    y = (xf - mu) * lax.rsqrt(var + NORM_EPS) * g.astype(jnp.float32) + b.astype(jnp.float32)
    return y.astype(x.dtype)


def modulate(h, shift, scale):
    return h * (1 + scale) + shift


def swiglu(x, wg, wu, wd):
    return (jax.nn.silu(x @ wg) * (x @ wu)) @ wd


def axial_rope(x, n_tokens):
    dh = x.shape[-1]
    quarter = dh // 4
    t = jnp.arange(n_tokens)
    pos = jnp.stack([t // GRID_W, t % GRID_W], -1).astype(jnp.float32)
    inv_freq = ROPE_BASE ** (-jnp.arange(quarter, dtype=jnp.float32) / quarter)
    ang = pos[:, :, None] * inv_freq
    cos, sin = jnp.cos(ang), jnp.sin(ang)
    xf = x.astype(jnp.float32).reshape(x.shape[:-1] + (2, 2, quarter))
    x1, x2 = xf[..., 0, :], xf[..., 1, :]
    out = jnp.stack([x1 * cos - x2 * sin, x2 * cos + x1 * sin], -2)
    return out.reshape(x.shape).astype(x.dtype)


def tiny_attention(q, k_lat, v_lat, k_ctx, v_ctx):
    b, s, t = q.shape
    k_all = jnp.concatenate([k_lat, k_ctx], 1)
    v_all = jnp.concatenate([v_lat, v_ctx], 1)
    q_blocks = q.reshape(b, s // ATTN_BLOCK, ATTN_BLOCK, t).transpose(1, 0, 2, 3)
    scale = t ** -0.5

    def one_block(qb):
        sc = jnp.einsum("bqt,bkt->bqk", qb, k_all, preferred_element_type=jnp.float32) * scale
        p = jax.nn.softmax(sc, -1).astype(v_all.dtype)
        return jnp.einsum("bqk,bkt->bqt", p, v_all)

    out = lax.map(one_block, q_blocks)
    return out.transpose(1, 0, 2, 3).reshape(b, s, t)


def spatial_gating(u, v, tiny, ln_g, ln_b, w_s, b_s):
    b, s, _ = v.shape
    vn = layer_norm(v, ln_g, ln_b).reshape(b, s // CHUNK, CHUNK, SGU_GROUPS, SGU_GROUP_DIM)
    mixed = jnp.einsum("gpq,bnqgc->bnpgc", w_s, vn) + b_s.T[:, :, None]
    return u * (mixed.reshape(b, s, SGU_WIDTH) + tiny)


def neighbourhood_attention(q, k, v, k_ctx, v_ctx, rpb, rows):
    b, s, h, dh = q.shape
    kh = min(NA_ROWS, rows)
    qg = q.reshape(b, rows, GRID_W, h, dh)
    kg = k.reshape(b, rows, GRID_W, h, dh)
    vg = v.reshape(b, rows, GRID_W, h, dh)
    cq = np.arange(GRID_W)
    c_start = np.clip(cq - NA_COLS // 2, 0, GRID_W - NA_COLS)
    col_mask = (cq[None, :] >= c_start[:, None]) & (cq[None, :] < c_start[:, None] + NA_COLS)
    band_mask = np.broadcast_to(col_mask[:, None, :], (GRID_W, kh, GRID_W)).reshape(GRID_W, kh * GRID_W)
    dc_idx = np.clip(cq[None, :] - cq[:, None], -(NA_COLS - 1), NA_COLS - 1) + NA_COLS - 1
    col_bias = rpb[:, :, dc_idx]
    scale = dh ** -0.5

    def one_row(r):
        r_start = jnp.clip(r - kh // 2, 0, rows - kh)
        qb = lax.dynamic_index_in_dim(qg, r, axis=1, keepdims=False)
        kb = lax.dynamic_slice_in_dim(kg, r_start, kh, axis=1).reshape(b, kh * GRID_W, h, dh)
        vb = lax.dynamic_slice_in_dim(vg, r_start, kh, axis=1).reshape(b, kh * GRID_W, h, dh)
        dr = r_start + jnp.arange(kh) - r + NA_ROWS - 1
        bias = jnp.take(col_bias, dr, axis=1).transpose(0, 2, 1, 3).reshape(h, GRID_W, kh * GRID_W)
        s_lat = jnp.einsum("bqhd,bkhd->bhqk", qb, kb, preferred_element_type=jnp.float32) * scale + bias
        s_lat = jnp.where(band_mask, s_lat, -jnp.inf)
        s_ctx = jnp.einsum("bqhd,bkhd->bhqk", qb, k_ctx, preferred_element_type=jnp.float32) * scale
        p = jax.nn.softmax(jnp.concatenate([s_lat, s_ctx], -1), -1).astype(v.dtype)
        n_lat = kh * GRID_W
        return (jnp.einsum("bhqk,bkhd->bqhd", p[..., :n_lat], vb)
                + jnp.einsum("bhqk,bkhd->bqhd", p[..., n_lat:], v_ctx))

    out = lax.map(one_row, jnp.arange(rows))
    return out.transpose(1, 0, 2, 3, 4).reshape(b, s, h * dh)


def moe_ffn(h, w_router, b_router, w_e_gate, w_e_up, w_e_down, w_sh_gate, w_sh_up, w_sh_down):
    n, d = h.shape
    scores = jax.nn.sigmoid(jnp.matmul(h, w_router, preferred_element_type=jnp.float32))
    _, idx = lax.top_k(scores + b_router.astype(jnp.float32), TOP_K)
    wts = jnp.take_along_axis(scores, idx, axis=-1)
    wts = (wts / jnp.sum(wts, -1, keepdims=True) * ROUTE_SCALE).astype(h.dtype)
    n_assign = n * TOP_K
    flat_e = idx.reshape(-1)
    flat_tok = jnp.repeat(jnp.arange(n, dtype=jnp.int32), TOP_K)
    flat_w = wts.reshape(-1)
    order = jnp.argsort(flat_e)
    se, st, sw = flat_e[order], flat_tok[order], flat_w[order]
    counts = jnp.bincount(flat_e, length=N_EXPERTS)
    starts = jnp.cumsum(counts) - counts
    pad_counts = (counts + EXPERT_BLOCK - 1) // EXPERT_BLOCK * EXPERT_BLOCK
    pad_ends = jnp.cumsum(pad_counts)
    pad_starts = pad_ends - pad_counts
    dest = pad_starts[se] + jnp.arange(n_assign) - starts[se]
    n_blocks = -(-n_assign // EXPERT_BLOCK) + N_EXPERTS
    n_rows = n_blocks * EXPERT_BLOCK
    row_tok = jnp.full((n_rows,), n, jnp.int32).at[dest].set(st)
    row_w = jnp.zeros((n_rows,), h.dtype).at[dest].set(sw)
    block_e = jnp.minimum(jnp.searchsorted(pad_ends, jnp.arange(n_blocks) * EXPERT_BLOCK, side="right"),
                          N_EXPERTS - 1)
    h_pad = jnp.concatenate([h, jnp.zeros((1, d), h.dtype)], 0)

    def expert_block(acc, blk):
        tok, wt, e = blk
        yb = swiglu(h_pad[tok], w_e_gate[e], w_e_up[e], w_e_down[e]) * wt[:, None]
        return acc.at[tok].add(yb), None

    routed, _ = lax.scan(expert_block, jnp.zeros((n + 1, d), h.dtype),
                         (row_tok.reshape(n_blocks, EXPERT_BLOCK), row_w.reshape(n_blocks, EXPERT_BLOCK), block_e))
    return swiglu(h, w_sh_gate, w_sh_up, w_sh_down) + routed[:n]


def hybrid_layer(x, c, ctx, c_ctx, w_ada, b_ada, norm_mix_g, norm_ffn_g, w_in, b_in,
                 sgu_ln_g, sgu_ln_b, sgu_w_s, sgu_b_s, tiny_w_o, na_rpb, w_up_a, w_up_b, w_out,
                 w_router, b_router, w_e_gate, w_e_up, w_e_down, w_sh_gate, w_sh_up, w_sh_down):
    b, s, d = x.shape
    rows = s // GRID_W
    ctx_len = ctx.shape[1]
    mod = (jax.nn.silu(c) @ w_ada + b_ada)[:, None, :]
    sh_mix, sc_mix, g_mix, sh_ffn, sc_ffn, g_ffn = jnp.split(mod, 6, -1)
    mod_ctx = jax.nn.silu(c_ctx) @ w_ada[:, :2 * d] + b_ada[:2 * d]

    h = modulate(rms_norm(x, norm_mix_g), sh_mix, sc_mix)
    p_in = h @ w_in + b_in
    u = jax.nn.gelu(p_in[..., COL_U:COL_V], approximate=False)
    v = jax.nn.gelu(p_in[..., COL_V:COL_TQ], approximate=False)
    tq = axial_rope(p_in[..., COL_TQ:COL_TK], s)
    tk = axial_rope(p_in[..., COL_TK:COL_TV], s)
    tv = p_in[..., COL_TV:COL_NQ]
    nq = p_in[..., COL_NQ:COL_NK].reshape(b, s, NA_HEADS, NA_HEAD_DIM)
    nk = p_in[..., COL_NK:COL_NV].reshape(b, s, NA_HEADS, NA_HEAD_DIM)
    nv = p_in[..., COL_NV:COL_GA].reshape(b, s, NA_HEADS, NA_HEAD_DIM)
    gate_a = jax.nn.sigmoid(p_in[..., COL_GA:COL_GB])
    gate_b = jax.nn.sigmoid(p_in[..., COL_GB:IN_WIDTH])

    hc = modulate(rms_norm(ctx, norm_mix_g), mod_ctx[:d], mod_ctx[d:])
    c_tiny = hc @ w_in[:, COL_TK:COL_NQ] + b_in[COL_TK:COL_NQ]
    tk_c, tv_c = c_tiny[..., :TINY_DIM], c_tiny[..., TINY_DIM:]
    c_na = hc @ w_in[:, COL_NK:COL_GA] + b_in[COL_NK:COL_GA]
    nk_c = c_na[..., :NA_WIDTH].reshape(b, ctx_len, NA_HEADS, NA_HEAD_DIM)
    nv_c = c_na[..., NA_WIDTH:].reshape(b, ctx_len, NA_HEADS, NA_HEAD_DIM)

    tiny = tiny_attention(tq, tk, tv, tk_c, tv_c) @ tiny_w_o
    o_a = spatial_gating(u, v, tiny, sgu_ln_g, sgu_ln_b, sgu_w_s, sgu_b_s)
    o_b = neighbourhood_attention(nq, nk, nv, nk_c, nv_c, na_rpb, rows)

    merged = gate_a * (o_a @ w_up_a) + gate_b * (o_b @ w_up_b)
    x = x + g_mix * (merged @ w_out)

    h2 = modulate(rms_norm(x, norm_ffn_g), sh_ffn, sc_ffn).reshape(b * s, d)
    y = moe_ffn(h2, w_router, b_router, w_e_gate, w_e_up, w_e_down, w_sh_gate, w_sh_up, w_sh_down)
    return x + g_ffn * y.reshape(b, s, d)


def setup_inputs(seed: int = 0) -> dict:
    key = jax.random.key(seed)
    ks = jax.random.split(key, 32)
    f32 = jnp.float32
    L, D = DEPTH, D_MODEL

    def nrm(k, shape, scale):
        return jax.random.normal(k, shape, f32) * scale

    return {
        "x": nrm(ks[0], (BATCH, SEQ, D), 1.0),
        "c": nrm(ks[1], (BATCH, D), 1.0),
        "ctx": nrm(ks[2], (BATCH, CTX_LEN, D), 1.0),
        "c_ctx": nrm(ks[3], (D,), 1.0),
        "w_ada": nrm(ks[4], (L, D, 6 * D), 0.5 * D ** -0.5),
        "b_ada": nrm(ks[5], (L, 6 * D), 0.01),
        "norm_mix_g": 1.0 + nrm(ks[6], (L, D), 0.1),
        "norm_ffn_g": 1.0 + nrm(ks[7], (L, D), 0.1),
        "w_in": nrm(ks[8], (L, D, IN_WIDTH), D ** -0.5),
        "b_in": nrm(ks[9], (L, IN_WIDTH), 0.01),
        "sgu_ln_g": 1.0 + nrm(ks[10], (L, SGU_WIDTH), 0.1),
        "sgu_ln_b": nrm(ks[11], (L, SGU_WIDTH), 0.01),
        "sgu_w_s": nrm(ks[12], (L, SGU_GROUPS, CHUNK, CHUNK), CHUNK ** -0.5),
        "sgu_b_s": 1.0 + nrm(ks[13], (L, SGU_GROUPS, CHUNK), 0.1),
        "tiny_w_o": nrm(ks[14], (L, TINY_DIM, SGU_WIDTH), TINY_DIM ** -0.5),
        "na_rpb": nrm(ks[15], (L, NA_HEADS, 2 * NA_ROWS - 1, 2 * NA_COLS - 1), 0.1),
        "w_up_a": nrm(ks[16], (L, SGU_WIDTH, D), SGU_WIDTH ** -0.5),
        "w_up_b": nrm(ks[17], (L, NA_WIDTH, D), NA_WIDTH ** -0.5),
        "w_out": nrm(ks[18], (L, D, D), D ** -0.5),
        "w_router": nrm(ks[19], (L, D, N_EXPERTS), D ** -0.5),
        "b_router": nrm(ks[20], (L, N_EXPERTS), 0.01),
        "w_e_gate": nrm(ks[21], (L, N_EXPERTS, D, EXPERT_FF), D ** -0.5),
        "w_e_up": nrm(ks[22], (L, N_EXPERTS, D, EXPERT_FF), D ** -0.5),
        "w_e_down": nrm(ks[23], (L, N_EXPERTS, EXPERT_FF, D), EXPERT_FF ** -0.5),
        "w_sh_gate": nrm(ks[24], (L, D, SHARED_FF), D ** -0.5),
        "w_sh_up": nrm(ks[25], (L, D, SHARED_FF), D ** -0.5),
        "w_sh_down": nrm(ks[26], (L, SHARED_FF, D), SHARED_FF ** -0.5),
        "final_norm_g": 1.0 + nrm(ks[27], (D,), 0.1),
    }


def reference(x, c, ctx, c_ctx, w_ada, b_ada, norm_mix_g, norm_ffn_g, w_in, b_in,
              sgu_ln_g, sgu_ln_b, sgu_w_s, sgu_b_s, tiny_w_o, na_rpb, w_up_a, w_up_b, w_out,
              w_router, b_router, w_e_gate, w_e_up, w_e_down, w_sh_gate, w_sh_up, w_sh_down,
              final_norm_g):
    for i in range(DEPTH):
        x = hybrid_layer(x, c, ctx, c_ctx, w_ada[i], b_ada[i], norm_mix_g[i], norm_ffn_g[i], w_in[i], b_in[i],
                         sgu_ln_g[i], sgu_ln_b[i], sgu_w_s[i], sgu_b_s[i], tiny_w_o[i], na_rpb[i],
                         w_up_a[i], w_up_b[i], w_out[i], w_router[i], b_router[i],
                         w_e_gate[i], w_e_up[i], w_e_down[i], w_sh_gate[i], w_sh_up[i], w_sh_down[i])
    return rms_norm(x, final_norm_g)
```

```python
import functools

import numpy as np
import jax
import jax.numpy as jnp
from jax import lax
from jax.experimental import pallas as pl
from jax.experimental.pallas import tpu as pltpu

F32 = jnp.float32
BF16 = jnp.bfloat16

D_MODEL = 1024
CTX_LEN = 256
GRID_W = 64
NORM_EPS = 1e-6
SGU_WIDTH = 1024
SGU_GROUPS = 8
SGU_GROUP_DIM = SGU_WIDTH // SGU_GROUPS
CHUNK = 128
TINY_DIM = 64
ROPE_BASE = 10000.0
NA_HEADS = 16
NA_HEAD_DIM = 32
NA_WIDTH = NA_HEADS * NA_HEAD_DIM
NA_ROWS = 8
NA_COLS = 16
N_EXPERTS = 64
TOP_K = 8
EXPERT_FF = 256
SHARED_FF = 256
ROUTE_SCALE = 2.5
COL_U = 0
COL_V = COL_U + SGU_WIDTH
COL_TQ = COL_V + SGU_WIDTH
COL_TK = COL_TQ + TINY_DIM
COL_TV = COL_TK + TINY_DIM
COL_NQ = COL_TV + TINY_DIM
COL_NK = COL_NQ + NA_WIDTH
COL_NV = COL_NK + NA_WIDTH
COL_GA = COL_NV + NA_WIDTH
COL_GB = COL_GA + D_MODEL
IN_WIDTH = COL_GB + D_MODEL

LANES = 128
SUBLANES = 8
VMEM_LIMIT = 56 * 1024 * 1024

TOKEN_TILE = 256
MOE_TILE = 1024
NA_GROUP = LANES // NA_HEAD_DIM


def _dot(a, b):
    return jnp.dot(a, b, preferred_element_type=F32)


def _dot_nt(a, b):
    return lax.dot_general(a, b, (((1,), (1,)), ((), ())), preferred_element_type=F32)


def _rms(x, g):
    return x * lax.rsqrt(jnp.mean(x * x, -1, keepdims=True) + NORM_EPS) * g


def _silu(x):
    return x * jax.nn.sigmoid(x)


def _gelu(x):
    return x * (lax.erf(x * np.float32(1.0 / np.sqrt(2.0))) + 1.0) * 0.5


def _params(*sem):
    return pltpu.CompilerParams(dimension_semantics=sem, vmem_limit_bytes=VMEM_LIMIT)


def _const_spec(shape):
    zeros = (0,) * len(shape)
    return pl.BlockSpec(shape, lambda *_: zeros)


def _ada_body(c_ref, w_ref, b_ref, o_ref):
    s = _silu(c_ref[...]).astype(BF16)
    o_ref[...] = _dot(s, w_ref[...].astype(BF16)) + b_ref[...]


def _ada(c_all, w_ada, b_ada):
    rows, d = c_all.shape
    width = w_ada.shape[1]
    return pl.pallas_call(
        _ada_body,
        grid=(width // d,),
        in_specs=[pl.BlockSpec((rows, d), lambda j: (0, 0)),
                  pl.BlockSpec((d, d), lambda j: (0, j)),
                  pl.BlockSpec((1, d), lambda j: (0, j))],
        out_specs=pl.BlockSpec((rows, d), lambda j: (0, j)),
        out_shape=jax.ShapeDtypeStruct((rows, width), F32),
        compiler_params=_params("arbitrary"),
        name="ada_mod",
    )(c_all, w_ada, b_ada.reshape(1, width))


def _inproj_body(x_ref, mod_ref, g_ref, wuv_ref, buv_ref, lng_ref, lnb_ref,
                 wt_ref, bt_ref, cos_ref, sin_ref, wna_ref, bna_ref, wg_ref, bg_ref,
                 u_ref, vn_ref, tq_ref, tk_ref, tv_ref, nq_ref, nk_ref, nv_ref,
                 ga_ref, gb_ref):
    y = _rms(x_ref[...], g_ref[...])
    h = (y * (1.0 + mod_ref[0, 1:2, :]) + mod_ref[0, 0:1, :]).astype(BF16)

    puv = _dot(h, wuv_ref[...]) + buv_ref[...]
    u_ref[...] = _gelu(puv[:, :SGU_WIDTH]).astype(BF16)
    v = _gelu(puv[:, SGU_WIDTH:])
    mu = jnp.mean(v, -1, keepdims=True)
    vc = v - mu
    var = jnp.mean(vc * vc, -1, keepdims=True)
    vn_ref[...] = (vc * lax.rsqrt(var + NORM_EPS) * lng_ref[...] + lnb_ref[...]).astype(BF16)

    pt = _dot(h, wt_ref[...]) + bt_ref[...]
    cos = cos_ref[...]
    sin = sin_ref[...]
    tq = pt[:, 0:LANES] * cos + pt[:, LANES:2 * LANES] * sin
    tq_ref[...] = (tq * np.float32(TINY_DIM ** -0.5)).astype(BF16)
    tk_ref[...] = (pt[:, 2 * LANES:3 * LANES] * cos + pt[:, 3 * LANES:4 * LANES] * sin).astype(BF16)
    tv_ref[...] = pt[:, 4 * LANES:5 * LANES].astype(BF16)

    pn = _dot(h, wna_ref[...]) + bna_ref[...]
    nq_ref[...] = (pn[:, :NA_WIDTH] * np.float32(NA_HEAD_DIM ** -0.5)).astype(BF16)
    nk_ref[...] = pn[:, NA_WIDTH:2 * NA_WIDTH].astype(BF16)
    nv_ref[...] = pn[:, 2 * NA_WIDTH:].astype(BF16)

    pg = _dot(h, wg_ref[...]) + bg_ref[...]
    ga_ref[...] = jax.nn.sigmoid(pg[:, :D_MODEL]).astype(BF16)
    gb_ref[...] = jax.nn.sigmoid(pg[:, D_MODEL:]).astype(BF16)


def _inproj(x2, mod3, g_mix, wuv, buv, lng, lnb, wt, bt, cos, sin, wna, bna, wg, bg, seq):
    n, d = x2.shape
    tm = TOKEN_TILE
    tpb = seq // tm
    tok = lambda w: pl.BlockSpec((tm, w), lambda i: (i, 0))
    pos = pl.BlockSpec((tm, LANES), lambda i: (i % tpb, 0))
    in_specs = [tok(d),
                pl.BlockSpec((1, 6, d), lambda i: (i // tpb, 0, 0)),
                _const_spec((1, d)),
                _const_spec(wuv.shape), _const_spec(buv.shape),
                _const_spec((1, d)), _const_spec((1, d)),
                _const_spec(wt.shape), _const_spec(bt.shape), pos, pos,
                _const_spec(wna.shape), _const_spec(bna.shape),
                _const_spec(wg.shape), _const_spec(bg.shape)]
    widths = [SGU_WIDTH, SGU_WIDTH, LANES, LANES, LANES, NA_WIDTH, NA_WIDTH, NA_WIDTH,
              D_MODEL, D_MODEL]
    return pl.pallas_call(
        _inproj_body,
        grid=(n // tm,),
        in_specs=in_specs,
        out_specs=[tok(w) for w in widths],
        out_shape=[jax.ShapeDtypeStruct((n, w), BF16) for w in widths],
        compiler_params=_params("arbitrary"),
        name="in_proj",
    )(x2, mod3, g_mix, wuv, buv, lng, lnb, wt, bt, cos, sin, wna, bna, wg, bg)


def _ctxproj_body(x_ref, mod_ref, g_ref, wt_ref, bt_ref, wna_ref, bna_ref,
                  tk_ref, tv_ref, nk_ref, nv_ref):
    y = _rms(x_ref[...], g_ref[...])
    h = (y * (1.0 + mod_ref[1:2, :]) + mod_ref[0:1, :]).astype(BF16)
    pt = _dot(h, wt_ref[...]) + bt_ref[...]
    tk_ref[...] = pt[:, :LANES].astype(BF16)
    tv_ref[...] = pt[:, LANES:].astype(BF16)
    pn = _dot(h, wna_ref[...]) + bna_ref[...]
    nk_ref[...] = pn[:, :NA_WIDTH].astype(BF16)
    nv_ref[...] = pn[:, NA_WIDTH:].astype(BF16)


def _ctxproj(ctx2, mod_ctx, g_mix, wt, bt, wna, bna):
    n, d = ctx2.shape
    tm = CTX_LEN
    tok = lambda w: pl.BlockSpec((tm, w), lambda i: (i, 0))
    widths = [LANES, LANES, NA_WIDTH, NA_WIDTH]
    return pl.pallas_call(
        _ctxproj_body,
        grid=(n // tm,),
        in_specs=[tok(d), _const_spec((2, d)), _const_spec((1, d)),
                  _const_spec(wt.shape), _const_spec(bt.shape),
                  _const_spec(wna.shape), _const_spec(bna.shape)],
        out_specs=[tok(w) for w in widths],
        out_shape=[jax.ShapeDtypeStruct((n, w), BF16) for w in widths],
        compiler_params=_params("arbitrary"),
        name="ctx_proj",
    )(ctx2, mod_ctx, g_mix, wt, bt, wna, bna)


def _mixer_a_body(tq_ref, tk_ref, tv_ref, tkc_ref, tvc_ref, vn_ref, u_ref,
                  ws_ref, bst_ref, two_ref, oa_ref):
    q = tq_ref[0]
    s_lat = _dot_nt(q, tk_ref[0])
    s_ctx = _dot_nt(q, tkc_ref[0])
    m = jnp.maximum(jnp.max(s_lat, -1, keepdims=True), jnp.max(s_ctx, -1, keepdims=True))
    p_lat = jnp.exp(s_lat - m)
    p_ctx = jnp.exp(s_ctx - m)
    denom = jnp.sum(p_lat, -1, keepdims=True) + jnp.sum(p_ctx, -1, keepdims=True)
    att = _dot(p_lat.astype(BF16), tv_ref[0]) + _dot(p_ctx.astype(BF16), tvc_ref[0])
    att = (att / denom).astype(BF16)
    tiny = _dot(att, two_ref[...])

    for g in range(SGU_GROUPS):
        cols = slice(g * SGU_GROUP_DIM, (g + 1) * SGU_GROUP_DIM)
        mixed = _dot(ws_ref[g], vn_ref[0, :, cols]) + bst_ref[:, g:g + 1]
        gate = mixed + tiny[:, cols]
        oa_ref[0, :, cols] = (u_ref[0, :, cols].astype(F32) * gate).astype(BF16)


def _mixer_a(tq, tk, tv, tkc, tvc, vn, u, ws, bst, two):
    b, s, _ = tq.shape
    blk = lambda w: pl.BlockSpec((1, CHUNK, w), lambda i, j: (i, j, 0))
    per_b = lambda rows, w: pl.BlockSpec((1, rows, w), lambda i, j: (i, 0, 0))
    return pl.pallas_call(
        _mixer_a_body,
        grid=(b, s // CHUNK),
        in_specs=[blk(LANES), per_b(s, LANES), per_b(s, LANES),
                  per_b(CTX_LEN, LANES), per_b(CTX_LEN, LANES),
                  blk(SGU_WIDTH), blk(SGU_WIDTH),
                  _const_spec(ws.shape), _const_spec(bst.shape), _const_spec(two.shape)],
        out_specs=blk(SGU_WIDTH),
        out_shape=jax.ShapeDtypeStruct((b, s, SGU_WIDTH), BF16),
        compiler_params=_params("arbitrary", "arbitrary"),
        name="mixer_a",
    )(tq, tk, tv, tkc, tvc, vn, u, ws, bst, two)


def _na_body(q_ref, k_ref, v_ref, kc_ref, vc_ref, bias_ref, o_ref, *, rows):
    r = pl.program_id(1)
    r_start = jnp.clip(r - NA_ROWS // 2, 0, rows - NA_ROWS)
    k0 = pl.multiple_of(r_start * GRID_W, GRID_W)
    n_lat = NA_ROWS * GRID_W
    lane = lax.broadcasted_iota(jnp.int32, (GRID_W, LANES), 1)
    for g in range(NA_WIDTH // LANES):
        cols = slice(g * LANES, (g + 1) * LANES)
        qg = q_ref[0, :, cols]
        kg = k_ref[0, pl.ds(k0, n_lat), cols]
        vg = v_ref[0, pl.ds(k0, n_lat), cols]
        kcg = kc_ref[0, :, cols]
        vcg = vc_ref[0, :, cols]
        acc = jnp.zeros((GRID_W, LANES), F32)
        for j in range(NA_GROUP):
            in_head = (lane >= j * NA_HEAD_DIM) & (lane < (j + 1) * NA_HEAD_DIM)
            qm = jnp.where(in_head, qg, jnp.zeros_like(qg))
            s_lat = _dot_nt(qm, kg) + bias_ref[0, g * NA_GROUP + j]
            s_ctx = _dot_nt(qm, kcg)
            m = jnp.maximum(jnp.max(s_lat, -1, keepdims=True),
                            jnp.max(s_ctx, -1, keepdims=True))
            p_lat = jnp.exp(s_lat - m)
            p_ctx = jnp.exp(s_ctx - m)
            denom = jnp.sum(p_lat, -1, keepdims=True) + jnp.sum(p_ctx, -1, keepdims=True)
            o = _dot(p_lat.astype(BF16), vg) + _dot(p_ctx.astype(BF16), vcg)
            acc = jnp.where(in_head, o / denom, acc)
        o_ref[0, :, cols] = acc.astype(BF16)


def _na(nq, nk, nv, nkc, nvc, bias):
    b, s, w = nq.shape
    rows = s // GRID_W
    kh = NA_ROWS

    def bias_idx(i, r):
        return (r - jnp.clip(r - kh // 2, 0, rows - kh), 0, 0, 0)

    per_b = lambda n: pl.BlockSpec((1, n, w), lambda i, r: (i, 0, 0))
    row = pl.BlockSpec((1, GRID_W, w), lambda i, r: (i, r, 0))
    return pl.pallas_call(
        functools.partial(_na_body, rows=rows),
        grid=(b, rows),
        in_specs=[row, per_b(s), per_b(s), per_b(CTX_LEN), per_b(CTX_LEN),
                  pl.BlockSpec((1, NA_HEADS, GRID_W, kh * GRID_W), bias_idx)],
        out_specs=row,
        out_shape=jax.ShapeDtypeStruct((b, s, w), BF16),
        compiler_params=_params("arbitrary", "arbitrary"),
        name="na_attn",
    )(nq, nk, nv, nkc, nvc, bias)


def _na_bias_table(rpb):
    cq = np.arange(GRID_W)
    c_start = np.clip(cq - NA_COLS // 2, 0, GRID_W - NA_COLS)
    col_mask = (cq[None, :] >= c_start[:, None]) & (cq[None, :] < c_start[:, None] + NA_COLS)
    dc_idx = np.clip(cq[None, :] - cq[:, None], -(NA_COLS - 1), NA_COLS - 1) + NA_COLS - 1
    col_bias = rpb[:, :, dc_idx]
    col_bias = jnp.where(col_mask[None, None], col_bias, -jnp.inf)
    off = np.arange(NA_ROWS)[:, None]
    dr = np.arange(NA_ROWS)[None, :] - off + NA_ROWS - 1
    tab = col_bias[:, dr]
    tab = tab.transpose(1, 0, 3, 2, 4)
    return tab.reshape(NA_ROWS, NA_HEADS, GRID_W, NA_ROWS * GRID_W).astype(F32)


def _merge_body(oa_ref, ob_ref, ga_ref, gb_ref, x_ref, mod_ref, wua_ref, wub_ref, wo_ref,
                gf_ref, wr_ref, br_ref, x1_ref, h2_ref, gate_ref):
    a = _dot(oa_ref[...], wua_ref[...])
    b = _dot(ob_ref[...], wub_ref[...])
    merged = ga_ref[...].astype(F32) * a + gb_ref[...].astype(F32) * b
    x1 = x_ref[...] + mod_ref[0, 2:3, :] * _dot(merged.astype(BF16), wo_ref[...])
    x1_ref[...] = x1
    h2 = _rms(x1, gf_ref[...]) * (1.0 + mod_ref[0, 4:5, :]) + mod_ref[0, 3:4, :]
    h2b = h2.astype(BF16)
    h2_ref[...] = h2b

    scores = jax.nn.sigmoid(_dot(h2b, wr_ref[...]))
    lane = lax.broadcasted_iota(jnp.int32, scores.shape, 1)
    sel = jnp.where(lane < N_EXPERTS, scores + br_ref[...], -jnp.inf)
    picked = jnp.zeros_like(scores)
    for _ in range(TOP_K):
        best = jnp.max(sel, -1, keepdims=True)
        first = jnp.min(jnp.where(sel == best, lane, LANES), -1, keepdims=True)
        hit = lane == first
        picked = jnp.where(hit, scores, picked)
        sel = jnp.where(hit, -jnp.inf, sel)
    gate_ref[...] = picked / jnp.sum(picked, -1, keepdims=True) * np.float32(ROUTE_SCALE)


def _merge(oa, ob, ga, gb, x2, mod3, wua, wub, wo, g_ffn_norm, wr, br, seq):
    n, d = x2.shape
    tm = TOKEN_TILE
    tpb = seq // tm
    tok = lambda w: pl.BlockSpec((tm, w), lambda i: (i, 0))
    return pl.pallas_call(
        _merge_body,
        grid=(n // tm,),
        in_specs=[tok(SGU_WIDTH), tok(NA_WIDTH), tok(d), tok(d), tok(d),
                  pl.BlockSpec((1, 6, d), lambda i: (i // tpb, 0, 0)),
                  _const_spec(wua.shape), _const_spec(wub.shape), _const_spec(wo.shape),
                  _const_spec((1, d)), _const_spec(wr.shape), _const_spec(br.shape)],
        out_specs=[tok(d), tok(d), tok(LANES)],
        out_shape=[jax.ShapeDtypeStruct((n, d), F32),
                   jax.ShapeDtypeStruct((n, d), BF16),
                   jax.ShapeDtypeStruct((n, LANES), F32)],
        compiler_params=_params("arbitrary"),
        name="merge_route",
    )(oa, ob, ga, gb, x2, mod3, wua, wub, wo, g_ffn_norm, wr, br)


def _moe_body(h_ref, gate_ref, x1_ref, mod_ref, wgu_ref, wd_ref, wsgu_ref, wsd_ref, fg_ref,
              o_ref, acc_ref):
    e = pl.program_id(1)
    h = h_ref[...]

    @pl.when(e == 0)
    def _():
        a = _dot(h, wsgu_ref[...])
        act = _silu(a[:, :SHARED_FF]) * a[:, SHARED_FF:]
        acc_ref[...] = _dot(act.astype(BF16), wsd_ref[...])

    a = _dot(h, wgu_ref[0])
    act = _silu(a[:, :EXPERT_FF]) * a[:, EXPERT_FF:]
    lane = lax.broadcasted_iota(jnp.int32, gate_ref.shape, 1)
    w_e = jnp.sum(jnp.where(lane == e, gate_ref[...], 0.0), -1, keepdims=True)
    acc_ref[...] += _dot((act * w_e).astype(BF16), wd_ref[0])

    @pl.when(e == pl.num_programs(1) - 1)
    def _():
        y = x1_ref[...] + mod_ref[0, 5:6, :] * acc_ref[...]
        o_ref[...] = _rms(y, fg_ref[...])


def _moe(h2, gate, x1, mod3, wgu, wd, wsgu, wsd, final_g, seq):
    n, d = h2.shape
    tm = min(MOE_TILE, seq)
    tpb = seq // tm
    tok = lambda w: pl.BlockSpec((tm, w), lambda i, e: (i, 0))
    return pl.pallas_call(
        _moe_body,
        grid=(n // tm, N_EXPERTS),
        in_specs=[tok(d), tok(LANES), tok(d),
                  pl.BlockSpec((1, 6, d), lambda i, e: (i // tpb, 0, 0)),
                  pl.BlockSpec((1, d, 2 * EXPERT_FF), lambda i, e: (e, 0, 0)),
                  pl.BlockSpec((1, EXPERT_FF, d), lambda i, e: (e, 0, 0)),
                  _const_spec(wsgu.shape), _const_spec(wsd.shape), _const_spec((1, d))],
        out_specs=tok(d),
        out_shape=jax.ShapeDtypeStruct((n, d), F32),
        scratch_shapes=[pltpu.VMEM((tm, d), F32)],
        compiler_params=_params("arbitrary", "arbitrary"),
        name="moe_ffn",
    )(h2, gate, x1, mod3, wgu, wd, wsgu, wsd, final_g)


def _pad_cols(w, width):
    return jnp.pad(w, ((0, 0), (0, width - w.shape[1])))


def _rope_tables(seq):
    quarter = TINY_DIM // 4
    t = jnp.arange(seq)
    pos = jnp.stack([t // GRID_W, t % GRID_W], -1).astype(F32)
    inv_freq = ROPE_BASE ** (-jnp.arange(quarter, dtype=F32) / quarter)
    ang = pos[:, :, None] * inv_freq
    cos, sin = jnp.cos(ang), jnp.sin(ang)
    cos_t = jnp.stack([cos, cos], 2).reshape(seq, TINY_DIM)
    sin_t = jnp.stack([-sin, sin], 2).reshape(seq, TINY_DIM)
    return _pad_cols(cos_t, LANES), _pad_cols(sin_t, LANES)


def _half_swap_perm():
    quarter = TINY_DIM // 4
    idx = np.arange(TINY_DIM).reshape(2, 2, quarter)
    return idx[:, ::-1, :].reshape(TINY_DIM)


def _layer(x, c, ctx, c_ctx, w_ada, b_ada, norm_mix_g, norm_ffn_g, w_in, b_in,
           sgu_ln_g, sgu_ln_b, sgu_w_s, sgu_b_s, tiny_w_o, na_rpb, w_up_a, w_up_b, w_out,
           w_router, b_router, w_e_gate, w_e_up, w_e_down, w_sh_gate, w_sh_up, w_sh_down,
           final_g):
    b, s, d = x.shape
    n = b * s
    row = lambda v: v.reshape(1, -1)

    n_mod = -(-(b + 1) // SUBLANES) * SUBLANES
    c_all = jnp.concatenate([c, c_ctx[None], jnp.zeros((n_mod - b - 1, d), F32)], 0)
    mod = _ada(c_all, w_ada, b_ada)
    mod3 = mod[:b].reshape(b, 6, d)
    mod_ctx = mod[b, :2 * d].reshape(2, d)

    perm = _half_swap_perm()
    wcols = lambda lo, hi: w_in[:, lo:hi]
    bcols = lambda lo, hi: b_in[lo:hi]
    wq, bq = wcols(COL_TQ, COL_TK), bcols(COL_TQ, COL_TK)
    wk, bk = wcols(COL_TK, COL_TV), bcols(COL_TK, COL_TV)
    wv, bv = wcols(COL_TV, COL_NQ), bcols(COL_TV, COL_NQ)
    padw = lambda w: _pad_cols(w, LANES)
    padb = lambda v: jnp.pad(v, (0, LANES - v.shape[0]))
    wt = jnp.concatenate([padw(wq), padw(wq[:, perm]), padw(wk), padw(wk[:, perm]), padw(wv)],
                         1).astype(BF16)
    bt = row(jnp.concatenate([padb(bq), padb(bq[perm]), padb(bk), padb(bk[perm]), padb(bv)]))
    wt_ctx = jnp.concatenate([padw(wk), padw(wv)], 1).astype(BF16)
    bt_ctx = row(jnp.concatenate([padb(bk), padb(bv)]))
    wuv, buv = wcols(COL_U, COL_TQ).astype(BF16), row(bcols(COL_U, COL_TQ))
    wna, bna = wcols(COL_NQ, COL_GA).astype(BF16), row(bcols(COL_NQ, COL_GA))
    wna_ctx, bna_ctx = wcols(COL_NK, COL_GA).astype(BF16), row(bcols(COL_NK, COL_GA))
    wg, bg = wcols(COL_GA, IN_WIDTH).astype(BF16), row(bcols(COL_GA, IN_WIDTH))
    cos, sin = _rope_tables(s)

    x2 = x.reshape(n, d)
    u, vn, tq, tk, tv, nq, nk, nv, ga, gb = _inproj(
        x2, mod3, row(norm_mix_g), wuv, buv, row(sgu_ln_g), row(sgu_ln_b), wt, bt, cos, sin,
        wna, bna, wg, bg, s)
    tkc, tvc, nkc, nvc = _ctxproj(ctx.reshape(b * CTX_LEN, d), mod_ctx, row(norm_mix_g),
                                  wt_ctx, bt_ctx, wna_ctx, bna_ctx)

    per_b = lambda a, rows: a.reshape(b, rows, a.shape[-1])
    two = jnp.pad(tiny_w_o, ((0, LANES - TINY_DIM), (0, 0))).astype(BF16)
    oa = _mixer_a(per_b(tq, s), per_b(tk, s), per_b(tv, s), per_b(tkc, CTX_LEN),
                  per_b(tvc, CTX_LEN), per_b(vn, s), per_b(u, s),
                  sgu_w_s.astype(BF16), sgu_b_s.T, two)
    ob = _na(per_b(nq, s), per_b(nk, s), per_b(nv, s), per_b(nkc, CTX_LEN),
             per_b(nvc, CTX_LEN), _na_bias_table(na_rpb))

    wr = _pad_cols(w_router, LANES).astype(BF16)
    br = row(jnp.pad(b_router, (0, LANES - N_EXPERTS)))
    x1, h2, gate = _merge(oa.reshape(n, SGU_WIDTH), ob.reshape(n, NA_WIDTH), ga, gb, x2, mod3,
                          w_up_a.astype(BF16), w_up_b.astype(BF16), w_out.astype(BF16),
                          row(norm_ffn_g), wr, br, s)

    wgu = jnp.concatenate([w_e_gate, w_e_up], -1).astype(BF16)
    wsgu = jnp.concatenate([w_sh_gate, w_sh_up], -1).astype(BF16)
    out = _moe(h2, gate, x1, mod3, wgu, w_e_down.astype(BF16), wsgu, w_sh_down.astype(BF16),
               row(final_g), s)
    return out.reshape(b, s, d)


def kernel(x, c, ctx, c_ctx, w_ada, b_ada, norm_mix_g, norm_ffn_g, w_in, b_in, sgu_ln_g, sgu_ln_b, sgu_w_s, sgu_b_s, tiny_w_o, na_rpb, w_up_a, w_up_b, w_out, w_router, b_router, w_e_gate, w_e_up, w_e_down, w_sh_gate, w_sh_up, w_sh_down, final_norm_g):
    assert w_ada.shape[0] == 1, "single-layer block"
    return _layer(x, c, ctx, c_ctx, w_ada[0], b_ada[0], norm_mix_g[0], norm_ffn_g[0], w_in[0],
                  b_in[0], sgu_ln_g[0], sgu_ln_b[0], sgu_w_s[0], sgu_b_s[0], tiny_w_o[0],
                  na_rpb[0], w_up_a[0], w_up_b[0], w_out[0], w_router[0], b_router[0],
                  w_e_gate[0], w_e_up[0], w_e_down[0], w_sh_gate[0], w_sh_up[0], w_sh_down[0],
                  final_norm_g)
```

```python
import functools

import numpy as np
import jax
import jax.numpy as jnp
from jax import lax
from jax.experimental import pallas as pl
from jax.experimental.pallas import tpu as pltpu

F32 = jnp.float32
BF16 = jnp.bfloat16

D_MODEL = 1024
CTX_LEN = 256
GRID_W = 64
NORM_EPS = 1e-6
SGU_WIDTH = 1024
SGU_GROUPS = 8
SGU_GROUP_DIM = SGU_WIDTH // SGU_GROUPS
CHUNK = 128
TINY_DIM = 64
ROPE_BASE = 10000.0
NA_HEADS = 16
NA_HEAD_DIM = 32
NA_WIDTH = NA_HEADS * NA_HEAD_DIM
NA_ROWS = 8
NA_COLS = 16
N_EXPERTS = 64
TOP_K = 8
EXPERT_FF = 256
SHARED_FF = 256
ROUTE_SCALE = 2.5
COL_U = 0
COL_V = COL_U + SGU_WIDTH
COL_TQ = COL_V + SGU_WIDTH
COL_TK = COL_TQ + TINY_DIM
COL_TV = COL_TK + TINY_DIM
COL_NQ = COL_TV + TINY_DIM
COL_NK = COL_NQ + NA_WIDTH
COL_NV = COL_NK + NA_WIDTH
COL_GA = COL_NV + NA_WIDTH
COL_GB = COL_GA + D_MODEL
IN_WIDTH = COL_GB + D_MODEL

LANES = 128
SUBLANES = 8
VMEM_LIMIT = 56 * 1024 * 1024

TOKEN_TILE = 256
NA_GROUP = LANES // NA_HEAD_DIM


def _dot(a, b):
    return jnp.dot(a, b, preferred_element_type=F32)


def _dot_nt(a, b):
    return lax.dot_general(a, b, (((1,), (1,)), ((), ())), preferred_element_type=F32)


def _rms(x, g):
    return x * lax.rsqrt(jnp.mean(x * x, -1, keepdims=True) + NORM_EPS) * g


def _silu(x):
    return x * jax.nn.sigmoid(x)


def _gelu(x):
    return x * (lax.erf(x * np.float32(1.0 / np.sqrt(2.0))) + 1.0) * 0.5


def _params(*sem):
    return pltpu.CompilerParams(dimension_semantics=sem, vmem_limit_bytes=VMEM_LIMIT)


def _const_spec(shape):
    zeros = (0,) * len(shape)
    return pl.BlockSpec(shape, lambda *_: zeros)


def _ada_body(c_ref, w_ref, b_ref, o_ref):
    s = _silu(c_ref[...]).astype(BF16)
    o_ref[...] = _dot(s, w_ref[...].astype(BF16)) + b_ref[...]


def _ada(c_all, w_ada, b_ada):
    rows, d = c_all.shape
    width = w_ada.shape[1]
    return pl.pallas_call(
        _ada_body,
        grid=(width // d,),
        in_specs=[pl.BlockSpec((rows, d), lambda j: (0, 0)),
                  pl.BlockSpec((d, d), lambda j: (0, j)),
                  pl.BlockSpec((1, d), lambda j: (0, j))],
        out_specs=pl.BlockSpec((rows, d), lambda j: (0, j)),
        out_shape=jax.ShapeDtypeStruct((rows, width), F32),
        compiler_params=_params("arbitrary"),
        name="ada_mod",
    )(c_all, w_ada, b_ada.reshape(1, width))


def _inproj_body(x_ref, mod_ref, g_ref, wuv_ref, buv_ref, lng_ref, lnb_ref,
                 wt_ref, bt_ref, cos_ref, sin_ref, wna_ref, bna_ref, wg_ref, bg_ref,
                 u_ref, vn_ref, tq_ref, tk_ref, tv_ref, nq_ref, nk_ref, nv_ref,
                 ga_ref, gb_ref):
    y = _rms(x_ref[...], g_ref[...])
    h = (y * (1.0 + mod_ref[0, 1:2, :]) + mod_ref[0, 0:1, :]).astype(BF16)

    puv = _dot(h, wuv_ref[...]) + buv_ref[...]
    u_ref[...] = _gelu(puv[:, :SGU_WIDTH]).astype(BF16)
    v = _gelu(puv[:, SGU_WIDTH:])
    mu = jnp.mean(v, -1, keepdims=True)
    vc = v - mu
    var = jnp.mean(vc * vc, -1, keepdims=True)
    vn_ref[...] = (vc * lax.rsqrt(var + NORM_EPS) * lng_ref[...] + lnb_ref[...]).astype(BF16)

    pt = _dot(h, wt_ref[...]) + bt_ref[...]
    cos = cos_ref[...]
    sin = sin_ref[...]
    tq = pt[:, 0:LANES] * cos + pt[:, LANES:2 * LANES] * sin
    tq_ref[...] = (tq * np.float32(TINY_DIM ** -0.5)).astype(BF16)
    tk_ref[...] = (pt[:, 2 * LANES:3 * LANES] * cos + pt[:, 3 * LANES:4 * LANES] * sin).astype(BF16)
    tv_ref[...] = pt[:, 4 * LANES:5 * LANES].astype(BF16)

    pn = _dot(h, wna_ref[...]) + bna_ref[...]
    nq_ref[...] = (pn[:, :NA_WIDTH] * np.float32(NA_HEAD_DIM ** -0.5)).astype(BF16)
    nk_ref[...] = pn[:, NA_WIDTH:2 * NA_WIDTH].astype(BF16)
    nv_ref[...] = pn[:, 2 * NA_WIDTH:].astype(BF16)

    pg = _dot(h, wg_ref[...]) + bg_ref[...]
    ga_ref[...] = jax.nn.sigmoid(pg[:, :D_MODEL]).astype(BF16)
    gb_ref[...] = jax.nn.sigmoid(pg[:, D_MODEL:]).astype(BF16)


def _inproj(x2, mod3, g_mix, wuv, buv, lng, lnb, wt, bt, cos, sin, wna, bna, wg, bg, seq):
    n, d = x2.shape
    tm = TOKEN_TILE
    tpb = seq // tm
    tok = lambda w: pl.BlockSpec((tm, w), lambda i: (i, 0))
    pos = pl.BlockSpec((tm, LANES), lambda i: (i % tpb, 0))
    in_specs = [tok(d),
                pl.BlockSpec((1, 6, d), lambda i: (i // tpb, 0, 0)),
                _const_spec((1, d)),
                _const_spec(wuv.shape), _const_spec(buv.shape),
                _const_spec((1, d)), _const_spec((1, d)),
                _const_spec(wt.shape), _const_spec(bt.shape), pos, pos,
                _const_spec(wna.shape), _const_spec(bna.shape),
                _const_spec(wg.shape), _const_spec(bg.shape)]
    widths = [SGU_WIDTH, SGU_WIDTH, LANES, LANES, LANES, NA_WIDTH, NA_WIDTH, NA_WIDTH,
              D_MODEL, D_MODEL]
    return pl.pallas_call(
        _inproj_body,
        grid=(n // tm,),
        in_specs=in_specs,
        out_specs=[tok(w) for w in widths],
        out_shape=[jax.ShapeDtypeStruct((n, w), BF16) for w in widths],
        compiler_params=_params("arbitrary"),
        name="in_proj",
    )(x2, mod3, g_mix, wuv, buv, lng, lnb, wt, bt, cos, sin, wna, bna, wg, bg)


def _ctxproj_body(x_ref, mod_ref, g_ref, wt_ref, bt_ref, wna_ref, bna_ref,
                  tk_ref, tv_ref, nk_ref, nv_ref):
    y = _rms(x_ref[...], g_ref[...])
    h = (y * (1.0 + mod_ref[1:2, :]) + mod_ref[0:1, :]).astype(BF16)
    pt = _dot(h, wt_ref[...]) + bt_ref[...]
    tk_ref[...] = pt[:, :LANES].astype(BF16)
    tv_ref[...] = pt[:, LANES:].astype(BF16)
    pn = _dot(h, wna_ref[...]) + bna_ref[...]
    nk_ref[...] = pn[:, :NA_WIDTH].astype(BF16)
    nv_ref[...] = pn[:, NA_WIDTH:].astype(BF16)


def _ctxproj(ctx2, mod_ctx, g_mix, wt, bt, wna, bna):
    n, d = ctx2.shape
    tm = CTX_LEN
    tok = lambda w: pl.BlockSpec((tm, w), lambda i: (i, 0))
    widths = [LANES, LANES, NA_WIDTH, NA_WIDTH]
    return pl.pallas_call(
        _ctxproj_body,
        grid=(n // tm,),
        in_specs=[tok(d), _const_spec((2, d)), _const_spec((1, d)),
                  _const_spec(wt.shape), _const_spec(bt.shape),
                  _const_spec(wna.shape), _const_spec(bna.shape)],
        out_specs=[tok(w) for w in widths],
        out_shape=[jax.ShapeDtypeStruct((n, w), BF16) for w in widths],
        compiler_params=_params("arbitrary"),
        name="ctx_proj",
    )(ctx2, mod_ctx, g_mix, wt, bt, wna, bna)


def _mixer_a_body(tq_ref, tk_ref, tv_ref, tkc_ref, tvc_ref, vn_ref, u_ref,
                  ws_ref, bst_ref, two_ref, oa_ref):
    q = tq_ref[0]
    s_lat = _dot_nt(q, tk_ref[0])
    s_ctx = _dot_nt(q, tkc_ref[0])
    m = jnp.maximum(jnp.max(s_lat, -1, keepdims=True), jnp.max(s_ctx, -1, keepdims=True))
    p_lat = jnp.exp(s_lat - m)
    p_ctx = jnp.exp(s_ctx - m)
    denom = jnp.sum(p_lat, -1, keepdims=True) + jnp.sum(p_ctx, -1, keepdims=True)
    att = _dot(p_lat.astype(BF16), tv_ref[0]) + _dot(p_ctx.astype(BF16), tvc_ref[0])
    att = (att / denom).astype(BF16)
    tiny = _dot(att, two_ref[...])

    for g in range(SGU_GROUPS):
        cols = slice(g * SGU_GROUP_DIM, (g + 1) * SGU_GROUP_DIM)
        mixed = _dot(ws_ref[g], vn_ref[0, :, cols]) + bst_ref[:, g:g + 1]
        gate = mixed + tiny[:, cols]
        oa_ref[0, :, cols] = (u_ref[0, :, cols].astype(F32) * gate).astype(BF16)


def _mixer_a(tq, tk, tv, tkc, tvc, vn, u, ws, bst, two):
    b, s, _ = tq.shape
    blk = lambda w: pl.BlockSpec((1, CHUNK, w), lambda i, j: (i, j, 0))
    per_b = lambda rows, w: pl.BlockSpec((1, rows, w), lambda i, j: (i, 0, 0))
    return pl.pallas_call(
        _mixer_a_body,
        grid=(b, s // CHUNK),
        in_specs=[blk(LANES), per_b(s, LANES), per_b(s, LANES),
                  per_b(CTX_LEN, LANES), per_b(CTX_LEN, LANES),
                  blk(SGU_WIDTH), blk(SGU_WIDTH),
                  _const_spec(ws.shape), _const_spec(bst.shape), _const_spec(two.shape)],
        out_specs=blk(SGU_WIDTH),
        out_shape=jax.ShapeDtypeStruct((b, s, SGU_WIDTH), BF16),
        compiler_params=_params("arbitrary", "arbitrary"),
        name="mixer_a",
    )(tq, tk, tv, tkc, tvc, vn, u, ws, bst, two)


def _na_body(q_ref, k_ref, v_ref, kc_ref, vc_ref, bias_ref, o_ref, *, rows):
    r = pl.program_id(1)
    r_start = jnp.clip(r - NA_ROWS // 2, 0, rows - NA_ROWS)
    k0 = pl.multiple_of(r_start * GRID_W, GRID_W)
    n_lat = NA_ROWS * GRID_W
    stacked = (NA_GROUP * GRID_W, LANES)
    own_lanes = (lax.broadcasted_iota(jnp.int32, stacked, 0) // GRID_W
                 == lax.broadcasted_iota(jnp.int32, stacked, 1) // NA_HEAD_DIM)
    for g in range(NA_WIDTH // LANES):
        cols = slice(g * LANES, (g + 1) * LANES)
        qg = q_ref[0, :, cols]
        qm = jnp.concatenate([qg] * NA_GROUP, 0)
        qm = jnp.where(own_lanes, qm, jnp.zeros_like(qm))
        bias = bias_ref[0, g * NA_GROUP:(g + 1) * NA_GROUP].reshape(stacked[0], n_lat)
        s_lat = _dot_nt(qm, k_ref[0, pl.ds(k0, n_lat), cols]) + bias
        s_ctx = _dot_nt(qm, kc_ref[0, :, cols])
        m = jnp.maximum(jnp.max(s_lat, -1, keepdims=True), jnp.max(s_ctx, -1, keepdims=True))
        p_lat = jnp.exp(s_lat - m)
        p_ctx = jnp.exp(s_ctx - m)
        denom = jnp.sum(p_lat, -1, keepdims=True) + jnp.sum(p_ctx, -1, keepdims=True)
        o = (_dot(p_lat.astype(BF16), v_ref[0, pl.ds(k0, n_lat), cols])
             + _dot(p_ctx.astype(BF16), vc_ref[0, :, cols]))
        o = jnp.where(own_lanes, o / denom, 0.0)
        acc = o[0:GRID_W]
        for j in range(1, NA_GROUP):
            acc = acc + o[j * GRID_W:(j + 1) * GRID_W]
        o_ref[0, :, cols] = acc.astype(BF16)


def _na(nq, nk, nv, nkc, nvc, bias):
    b, s, w = nq.shape
    rows = s // GRID_W
    kh = NA_ROWS

    def bias_idx(i, r):
        return (r - jnp.clip(r - kh // 2, 0, rows - kh), 0, 0, 0)

    per_b = lambda n: pl.BlockSpec((1, n, w), lambda i, r: (i, 0, 0))
    row = pl.BlockSpec((1, GRID_W, w), lambda i, r: (i, r, 0))
    return pl.pallas_call(
        functools.partial(_na_body, rows=rows),
        grid=(b, rows),
        in_specs=[row, per_b(s), per_b(s), per_b(CTX_LEN), per_b(CTX_LEN),
                  pl.BlockSpec((1, NA_HEADS, GRID_W, kh * GRID_W), bias_idx)],
        out_specs=row,
        out_shape=jax.ShapeDtypeStruct((b, s, w), BF16),
        compiler_params=_params("arbitrary", "arbitrary"),
        name="na_attn",
    )(nq, nk, nv, nkc, nvc, bias)


def _na_bias_table(rpb):
    cq = np.arange(GRID_W)
    c_start = np.clip(cq - NA_COLS // 2, 0, GRID_W - NA_COLS)
    col_mask = (cq[None, :] >= c_start[:, None]) & (cq[None, :] < c_start[:, None] + NA_COLS)
    dc_idx = np.clip(cq[None, :] - cq[:, None], -(NA_COLS - 1), NA_COLS - 1) + NA_COLS - 1
    col_bias = rpb[:, :, dc_idx]
    col_bias = jnp.where(col_mask[None, None], col_bias, -jnp.inf)
    off = np.arange(NA_ROWS)[:, None]
    dr = np.arange(NA_ROWS)[None, :] - off + NA_ROWS - 1
    tab = col_bias[:, dr]
    tab = tab.transpose(1, 0, 3, 2, 4)
    return tab.reshape(NA_ROWS, NA_HEADS, GRID_W, NA_ROWS * GRID_W).astype(F32)


def _merge_body(oa_ref, ob_ref, ga_ref, gb_ref, x_ref, mod_ref, wua_ref, wub_ref, wo_ref,
                gf_ref, wr_ref, br_ref, x1_ref, h2_ref, gate_ref, slot_ref):
    a = _dot(oa_ref[...], wua_ref[...])
    b = _dot(ob_ref[...], wub_ref[...])
    merged = ga_ref[...].astype(F32) * a + gb_ref[...].astype(F32) * b
    x1 = x_ref[...] + mod_ref[0, 2:3, :] * _dot(merged.astype(BF16), wo_ref[...])
    x1_ref[...] = x1
    h2 = _rms(x1, gf_ref[...]) * (1.0 + mod_ref[0, 4:5, :]) + mod_ref[0, 3:4, :]
    h2b = h2.astype(BF16)
    h2_ref[...] = h2b

    scores = jax.nn.sigmoid(_dot(h2b, wr_ref[...]))
    lane = lax.broadcasted_iota(jnp.int32, scores.shape, 1)
    sel = jnp.where(lane < N_EXPERTS, scores + br_ref[...], -jnp.inf)
    picked = jnp.zeros_like(scores)
    for _ in range(TOP_K):
        best = jnp.max(sel, -1, keepdims=True)
        first = jnp.min(jnp.where(sel == best, lane, LANES), -1, keepdims=True)
        hit = lane == first
        picked = jnp.where(hit, scores, picked)
        sel = jnp.where(hit, -jnp.inf, sel)
    gate_ref[...] = picked / jnp.sum(picked, -1, keepdims=True) * np.float32(ROUTE_SCALE)

    routed = picked > 0.0
    tm = scores.shape[0]
    earlier = (lax.broadcasted_iota(jnp.int32, (tm, tm), 1)
               < lax.broadcasted_iota(jnp.int32, (tm, tm), 0))
    slot = _dot(earlier.astype(BF16), routed.astype(BF16))
    slot_ref[...] = jnp.where(routed, slot, -1.0)


def _merge(oa, ob, ga, gb, x2, mod3, wua, wub, wo, g_ffn_norm, wr, br, seq):
    n, d = x2.shape
    tm = TOKEN_TILE
    tpb = seq // tm
    tok = lambda w: pl.BlockSpec((tm, w), lambda i: (i, 0))
    return pl.pallas_call(
        _merge_body,
        grid=(n // tm,),
        in_specs=[tok(SGU_WIDTH), tok(NA_WIDTH), tok(d), tok(d), tok(d),
                  pl.BlockSpec((1, 6, d), lambda i: (i // tpb, 0, 0)),
                  _const_spec(wua.shape), _const_spec(wub.shape), _const_spec(wo.shape),
                  _const_spec((1, d)), _const_spec(wr.shape), _const_spec(br.shape)],
        out_specs=[tok(d), tok(d), tok(LANES), tok(LANES)],
        out_shape=[jax.ShapeDtypeStruct((n, d), F32),
                   jax.ShapeDtypeStruct((n, d), BF16),
                   jax.ShapeDtypeStruct((n, LANES), F32),
                   jax.ShapeDtypeStruct((n, LANES), F32)],
        compiler_params=_params("arbitrary"),
        name="merge_route",
    )(oa, ob, ga, gb, x2, mod3, wua, wub, wo, g_ffn_norm, wr, br)


SLAB = 16
EXPERT_BLOCK = 512
LOCAL_ROWS = 3072
ROW_CHUNK = 1024
assert LOCAL_ROWS >= TOKEN_TILE * TOP_K + N_EXPERTS * (SLAB - 1) and LOCAL_ROWS % ROW_CHUNK == 0
TAB_SIZE, TAB_OFF, TAB_BASE = 0, N_EXPERTS, 2 * N_EXPERTS


def _segment_tables(slot, n_tiles):
    routed = (slot[:, :N_EXPERTS] >= 0).reshape(n_tiles, TOKEN_TILE, N_EXPERTS)
    cnt = jnp.sum(routed.astype(jnp.int32), 1)
    size = (cnt + SLAB - 1) // SLAB
    off = jnp.cumsum(size, 1) - size
    total = jnp.sum(size, 0)
    per_block = EXPERT_BLOCK // SLAB
    blocks = (total + per_block - 1) // per_block
    block_end = jnp.cumsum(blocks)
    block_start = block_end - blocks
    base = block_start[None, :] * per_block + jnp.cumsum(size, 0) - size
    table = jnp.concatenate([size, off, base], 1).astype(jnp.int32)[:, None, :]

    n_blocks = -(-(n_tiles * (LOCAL_ROWS // SLAB)) // per_block) + N_EXPERTS
    n_used = block_end[-1]
    b = jnp.minimum(jnp.arange(n_blocks), n_used - 1)
    expert = jnp.minimum(jnp.searchsorted(block_end, b, side="right"), N_EXPERTS - 1)
    valid = jnp.clip(total[expert] * SLAB - (b - block_start[expert]) * EXPERT_BLOCK,
                     0, EXPERT_BLOCK)
    return (table, b.astype(jnp.int32), expert.astype(jnp.int32), valid.astype(jnp.int32),
            n_used.astype(jnp.int32).reshape(1), n_blocks)


def _for_each_slab(tab_ref, fn):
    def per_expert(e, carry):
        size = tab_ref[0, 0, TAB_SIZE + e]
        off = tab_ref[0, 0, TAB_OFF + e]
        base = tab_ref[0, 0, TAB_BASE + e]

        def per_slab(q, c):
            fn(pl.multiple_of((off + q) * SLAB, SLAB), pl.multiple_of((base + q) * SLAB, SLAB))
            return c
        return lax.fori_loop(0, size, per_slab, carry)
    lax.fori_loop(0, N_EXPERTS, per_expert, 0)


def _tile_slabs(tab_ref):
    return tab_ref[0, 0, TAB_OFF + N_EXPERTS - 1] + tab_ref[0, 0, TAB_SIZE + N_EXPERTS - 1]


def _slab_sizes(count):
    return jnp.floor((count + (SLAB - 1.0)) * (1.0 / SLAB)) * SLAB


def _dispatch_body(tab_ref, slot_ref, h_ref, xs_ref, xbuf_ref, sem_ref, pending_ref):
    j = pl.program_id(0)
    cur = j % 2
    slot = slot_ref[...]
    count = jnp.sum((slot >= 0.0).astype(F32), 0, keepdims=True)
    size = _slab_sizes(count)
    earlier = (lax.broadcasted_iota(jnp.int32, (LANES, LANES), 0)
               < lax.broadcasted_iota(jnp.int32, (LANES, LANES), 1))
    off = _dot(jnp.broadcast_to(size, (SUBLANES, LANES)).astype(BF16),
               earlier.astype(BF16))[0:1]
    slot_t = slot.T.astype(BF16)
    h = h_ref[...]
    for c in range(LOCAL_ROWS // ROW_CHUNK):
        r = (lax.broadcasted_iota(jnp.int32, (ROW_CHUNK, LANES), 0) + c * ROW_CHUNK).astype(F32)
        seg = (r >= off) & (r < off + size)
        pos = jnp.sum(jnp.where(seg, r - off + 1.0, 0.0), 1, keepdims=True) - 1.0
        want = _dot(jnp.where(seg, 1.0, 0.0).astype(BF16), slot_t)
        onehot = jnp.where(want == pos, 1.0, 0.0).astype(BF16)
        xbuf_ref[cur, c * ROW_CHUNK:(c + 1) * ROW_CHUNK, :] = _dot(onehot, h).astype(BF16)

    def slab_copy(buf, local_row, global_row):
        return pltpu.make_async_copy(xbuf_ref.at[buf, pl.ds(local_row, SLAB), :],
                                     xs_ref.at[pl.ds(global_row, SLAB), :], sem_ref.at[buf])

    _for_each_slab(tab_ref, lambda lr, gr: slab_copy(cur, lr, gr).start())

    def drain(buf, n):
        lax.fori_loop(0, n, lambda q, c: (slab_copy(buf, 0, 0).wait(), c)[1], 0)

    @pl.when(j > 0)
    def _():
        drain(1 - cur, pending_ref[0])

    pending_ref[0] = _tile_slabs(tab_ref)

    @pl.when(j == pl.num_programs(0) - 1)
    def _():
        drain(cur, pending_ref[0])


def _dispatch(table, slot, h2, n_rows):
    n, d = h2.shape
    tm = TOKEN_TILE
    return pl.pallas_call(
        _dispatch_body,
        grid=(n // tm,),
        in_specs=[pl.BlockSpec((1, 1, 3 * N_EXPERTS), lambda i: (i, 0, 0),
                               memory_space=pltpu.SMEM),
                  pl.BlockSpec((tm, LANES), lambda i: (i, 0)),
                  pl.BlockSpec((tm, d), lambda i: (i, 0))],
        out_specs=pl.BlockSpec(memory_space=pl.ANY),
        out_shape=jax.ShapeDtypeStruct((n_rows, d), BF16),
        scratch_shapes=[pltpu.VMEM((2, LOCAL_ROWS, d), BF16),
                        pltpu.SemaphoreType.DMA((2,)),
                        pltpu.SMEM((1,), jnp.int32)],
        compiler_params=_params("arbitrary"),
        name="moe_dispatch",
    )(table, slot, h2)


def _expert_body(blk_ref, exp_ref, valid_ref, used_ref, x_ref, wgu_ref, wd_ref, y_ref):
    b = pl.program_id(0)

    @pl.when(b < used_ref[0])
    def _():
        row = lax.broadcasted_iota(jnp.int32, x_ref.shape, 0)
        x = jnp.where(row < valid_ref[b], x_ref[...], jnp.zeros_like(x_ref))
        a = _dot(x, wgu_ref[0])
        act = _silu(a[:, :EXPERT_FF]) * a[:, EXPERT_FF:]
        y_ref[...] = _dot(act.astype(BF16), wd_ref[0]).astype(BF16)


def _experts(xs, block, expert, valid, n_used, n_blocks, wgu, wd):
    _, d = xs.shape
    rows = pl.BlockSpec((EXPERT_BLOCK, d), lambda b, blk, exp, val, used: (blk[b], 0))
    grid_spec = pltpu.PrefetchScalarGridSpec(
        num_scalar_prefetch=4,
        grid=(n_blocks,),
        in_specs=[rows,
                  pl.BlockSpec((1, d, 2 * EXPERT_FF), lambda b, blk, exp, val, used: (exp[b], 0, 0)),
                  pl.BlockSpec((1, EXPERT_FF, d), lambda b, blk, exp, val, used: (exp[b], 0, 0))],
        out_specs=rows)
    return pl.pallas_call(
        _expert_body,
        grid_spec=grid_spec,
        out_shape=jax.ShapeDtypeStruct(xs.shape, BF16),
        compiler_params=_params("arbitrary"),
        name="moe_experts",
    )(block, expert, valid, n_used, xs, wgu, wd)


def _combine_body(tab_ref, nxt_ref, gate_ref, slot_ref, h_ref, x1_ref, mod_ref, wsgu_ref,
                  wsd_ref, fg_ref, ys_ref, o_ref, pw_ref, ybuf_ref, sem_ref):
    j = pl.program_id(0)
    cur = j % 2

    def slab_copy(buf, local_row, global_row):
        return pltpu.make_async_copy(ys_ref.at[pl.ds(global_row, SLAB), :],
                                     ybuf_ref.at[buf, pl.ds(local_row, SLAB), :], sem_ref.at[buf])

    @pl.when(j == 0)
    def _():
        ybuf_ref[...] = jnp.zeros_like(ybuf_ref)
        _for_each_slab(tab_ref, lambda lr, gr: slab_copy(0, lr, gr).start())

    @pl.when(j < pl.num_programs(0) - 1)
    def _():
        _for_each_slab(nxt_ref, lambda lr, gr: slab_copy(1 - cur, lr, gr).start())

    slot = slot_ref[...]
    slot_t = slot.T
    count = jnp.sum((slot_t >= 0.0).astype(F32), 1, keepdims=True)
    size = _slab_sizes(count)
    earlier = (lax.broadcasted_iota(jnp.int32, (LANES, LANES), 1)
               < lax.broadcasted_iota(jnp.int32, (LANES, LANES), 0))
    off = _dot(earlier.astype(BF16),
               jnp.broadcast_to(size, (LANES, LANES)).astype(BF16))[:, 0:1]
    slot_b = slot.astype(BF16)
    gate_b = gate_ref[...].astype(BF16)
    for c in range(LOCAL_ROWS // ROW_CHUNK):
        r = (lax.broadcasted_iota(jnp.int32, (LANES, ROW_CHUNK), 1) + c * ROW_CHUNK).astype(F32)
        seg = (r >= off) & (r < off + size)
        pos = jnp.sum(jnp.where(seg, r - off + 1.0, 0.0), 0, keepdims=True) - 1.0
        seg_b = jnp.where(seg, 1.0, 0.0).astype(BF16)
        want = _dot(slot_b, seg_b)
        weight = _dot(gate_b, seg_b)
        pw_ref[:, c * ROW_CHUNK:(c + 1) * ROW_CHUNK] = jnp.where(want == pos, weight,
                                                                 0.0).astype(BF16)

    a = _dot(h_ref[...], wsgu_ref[...])
    act = _silu(a[:, :SHARED_FF]) * a[:, SHARED_FF:]
    shared = _dot(act.astype(BF16), wsd_ref[...])

    lax.fori_loop(0, _tile_slabs(tab_ref), lambda q, c: (slab_copy(cur, 0, 0).wait(), c)[1], 0)
    routed = _dot(pw_ref[...], ybuf_ref[cur])
    y = x1_ref[...] + mod_ref[0, 5:6, :] * (shared + routed)
    o_ref[...] = _rms(y, fg_ref[...])


def _combine(table, ys, gate, slot, h2, x1, mod3, wsgu, wsd, final_g, seq):
    n, d = h2.shape
    tm = TOKEN_TILE
    tpb = seq // tm
    last = n // tm - 1
    tok = lambda w: pl.BlockSpec((tm, w), lambda i: (i, 0))
    tab = lambda f: pl.BlockSpec((1, 1, 3 * N_EXPERTS), lambda i: (f(i), 0, 0),
                                 memory_space=pltpu.SMEM)
    return pl.pallas_call(
        _combine_body,
        grid=(n // tm,),
        in_specs=[tab(lambda i: i), tab(lambda i: jnp.minimum(i + 1, last)),
                  tok(LANES), tok(LANES), tok(d), tok(d),
                  pl.BlockSpec((1, 6, d), lambda i: (i // tpb, 0, 0)),
                  _const_spec(wsgu.shape), _const_spec(wsd.shape), _const_spec((1, d)),
                  pl.BlockSpec(memory_space=pl.ANY)],
        out_specs=tok(d),
        out_shape=jax.ShapeDtypeStruct((n, d), F32),
        scratch_shapes=[pltpu.VMEM((tm, LOCAL_ROWS), BF16),
                        pltpu.VMEM((2, LOCAL_ROWS, d), BF16),
                        pltpu.SemaphoreType.DMA((2,))],
        compiler_params=_params("arbitrary"),
        name="moe_combine",
    )(table, table, gate, slot, h2, x1, mod3, wsgu, wsd, final_g, ys)


def _moe(h2, gate, slot, x1, mod3, wgu, wd, wsgu, wsd, final_g, seq):
    n, _ = h2.shape
    table, block, expert, valid, n_used, n_blocks = _segment_tables(slot, n // TOKEN_TILE)
    xs = _dispatch(table, slot, h2, n_blocks * EXPERT_BLOCK)
    ys = _experts(xs, block, expert, valid, n_used, n_blocks, wgu, wd)
    return _combine(table, ys, gate, slot, h2, x1, mod3, wsgu, wsd, final_g, seq)


def _pad_cols(w, width):
    return jnp.pad(w, ((0, 0), (0, width - w.shape[1])))


def _rope_tables(seq):
    quarter = TINY_DIM // 4
    t = jnp.arange(seq)
    pos = jnp.stack([t // GRID_W, t % GRID_W], -1).astype(F32)
    inv_freq = ROPE_BASE ** (-jnp.arange(quarter, dtype=F32) / quarter)
    ang = pos[:, :, None] * inv_freq
    cos, sin = jnp.cos(ang), jnp.sin(ang)
    cos_t = jnp.stack([cos, cos], 2).reshape(seq, TINY_DIM)
    sin_t = jnp.stack([-sin, sin], 2).reshape(seq, TINY_DIM)
    return _pad_cols(cos_t, LANES), _pad_cols(sin_t, LANES)


def _half_swap_perm():
    quarter = TINY_DIM // 4
    idx = np.arange(TINY_DIM).reshape(2, 2, quarter)
    return idx[:, ::-1, :].reshape(TINY_DIM)


def _layer(x, c, ctx, c_ctx, w_ada, b_ada, norm_mix_g, norm_ffn_g, w_in, b_in,
           sgu_ln_g, sgu_ln_b, sgu_w_s, sgu_b_s, tiny_w_o, na_rpb, w_up_a, w_up_b, w_out,
           w_router, b_router, w_e_gate, w_e_up, w_e_down, w_sh_gate, w_sh_up, w_sh_down,
           final_g):
    b, s, d = x.shape
    n = b * s
    row = lambda v: v.reshape(1, -1)

    n_mod = -(-(b + 1) // SUBLANES) * SUBLANES
    c_all = jnp.concatenate([c, c_ctx[None], jnp.zeros((n_mod - b - 1, d), F32)], 0)
    mod = _ada(c_all, w_ada, b_ada)
    mod3 = mod[:b].reshape(b, 6, d)
    mod_ctx = mod[b, :2 * d].reshape(2, d)

    perm = _half_swap_perm()
    wcols = lambda lo, hi: w_in[:, lo:hi]
    bcols = lambda lo, hi: b_in[lo:hi]
    wq, bq = wcols(COL_TQ, COL_TK), bcols(COL_TQ, COL_TK)
    wk, bk = wcols(COL_TK, COL_TV), bcols(COL_TK, COL_TV)
    wv, bv = wcols(COL_TV, COL_NQ), bcols(COL_TV, COL_NQ)
    padw = lambda w: _pad_cols(w, LANES)
    padb = lambda v: jnp.pad(v, (0, LANES - v.shape[0]))
    wt = jnp.concatenate([padw(wq), padw(wq[:, perm]), padw(wk), padw(wk[:, perm]), padw(wv)],
                         1).astype(BF16)
    bt = row(jnp.concatenate([padb(bq), padb(bq[perm]), padb(bk), padb(bk[perm]), padb(bv)]))
    wt_ctx = jnp.concatenate([padw(wk), padw(wv)], 1).astype(BF16)
    bt_ctx = row(jnp.concatenate([padb(bk), padb(bv)]))
    wuv, buv = wcols(COL_U, COL_TQ).astype(BF16), row(bcols(COL_U, COL_TQ))
    wna, bna = wcols(COL_NQ, COL_GA).astype(BF16), row(bcols(COL_NQ, COL_GA))
    wna_ctx, bna_ctx = wcols(COL_NK, COL_GA).astype(BF16), row(bcols(COL_NK, COL_GA))
    wg, bg = wcols(COL_GA, IN_WIDTH).astype(BF16), row(bcols(COL_GA, IN_WIDTH))
    cos, sin = _rope_tables(s)

    x2 = x.reshape(n, d)
    u, vn, tq, tk, tv, nq, nk, nv, ga, gb = _inproj(
        x2, mod3, row(norm_mix_g), wuv, buv, row(sgu_ln_g), row(sgu_ln_b), wt, bt, cos, sin,
        wna, bna, wg, bg, s)
    tkc, tvc, nkc, nvc = _ctxproj(ctx.reshape(b * CTX_LEN, d), mod_ctx, row(norm_mix_g),
                                  wt_ctx, bt_ctx, wna_ctx, bna_ctx)

    per_b = lambda a, rows: a.reshape(b, rows, a.shape[-1])
    two = jnp.pad(tiny_w_o, ((0, LANES - TINY_DIM), (0, 0))).astype(BF16)
    oa = _mixer_a(per_b(tq, s), per_b(tk, s), per_b(tv, s), per_b(tkc, CTX_LEN),
                  per_b(tvc, CTX_LEN), per_b(vn, s), per_b(u, s),
                  sgu_w_s.astype(BF16), sgu_b_s.T, two)
    ob = _na(per_b(nq, s), per_b(nk, s), per_b(nv, s), per_b(nkc, CTX_LEN),
             per_b(nvc, CTX_LEN), _na_bias_table(na_rpb))

    wr = _pad_cols(w_router, LANES).astype(BF16)
    br = row(jnp.pad(b_router, (0, LANES - N_EXPERTS)))
    x1, h2, gate, slot = _merge(oa.reshape(n, SGU_WIDTH), ob.reshape(n, NA_WIDTH), ga, gb, x2,
                                mod3, w_up_a.astype(BF16), w_up_b.astype(BF16),
                                w_out.astype(BF16), row(norm_ffn_g), wr, br, s)

    wgu = jnp.concatenate([w_e_gate, w_e_up], -1).astype(BF16)
    wd = w_e_down.astype(BF16)
    wsgu = jnp.concatenate([w_sh_gate, w_sh_up], -1).astype(BF16)
    wsd = w_sh_down.astype(BF16)
    out = _moe(h2, gate, slot, x1, mod3, wgu, wd, wsgu, wsd, row(final_g), s)
    return out.reshape(b, s, d)


def kernel(x, c, ctx, c_ctx, w_ada, b_ada, norm_mix_g, norm_ffn_g, w_in, b_in, sgu_ln_g, sgu_ln_b, sgu_w_s, sgu_b_s, tiny_w_o, na_rpb, w_up_a, w_up_b, w_out, w_router, b_router, w_e_gate, w_e_up, w_e_down, w_sh_gate, w_sh_up, w_sh_down, final_norm_g):
    assert w_ada.shape[0] == 1, "single-layer block"
    return _layer(x, c, ctx, c_ctx, w_ada[0], b_ada[0], norm_mix_g[0], norm_ffn_g[0], w_in[0],
                  b_in[0], sgu_ln_g[0], sgu_ln_b[0], sgu_w_s[0], sgu_b_s[0], tiny_w_o[0],
                  na_rpb[0], w_up_a[0], w_up_b[0], w_out[0], w_router[0], b_router[0],
                  w_e_gate[0], w_e_up[0], w_e_down[0], w_sh_gate[0], w_sh_up[0], w_sh_down[0],
                  final_norm_g)
```

```python
import functools

import numpy as np
import jax
import jax.numpy as jnp
from jax import lax
from jax.experimental import pallas as pl
from jax.experimental.pallas import tpu as pltpu

F32 = jnp.float32
BF16 = jnp.bfloat16

D_MODEL = 1024
CTX_LEN = 256
GRID_W = 64
NORM_EPS = 1e-6
SGU_WIDTH = 1024
SGU_GROUPS = 8
SGU_GROUP_DIM = SGU_WIDTH // SGU_GROUPS
CHUNK = 128
TINY_DIM = 64
ROPE_BASE = 10000.0
NA_HEADS = 16
NA_HEAD_DIM = 32
NA_WIDTH = NA_HEADS * NA_HEAD_DIM
NA_ROWS = 8
NA_COLS = 16
N_EXPERTS = 64
TOP_K = 8
EXPERT_FF = 256
SHARED_FF = 256
ROUTE_SCALE = 2.5
COL_U = 0
COL_V = COL_U + SGU_WIDTH
COL_TQ = COL_V + SGU_WIDTH
COL_TK = COL_TQ + TINY_DIM
COL_TV = COL_TK + TINY_DIM
COL_NQ = COL_TV + TINY_DIM
COL_NK = COL_NQ + NA_WIDTH
COL_NV = COL_NK + NA_WIDTH
COL_GA = COL_NV + NA_WIDTH
COL_GB = COL_GA + D_MODEL
IN_WIDTH = COL_GB + D_MODEL

LANES = 128
SUBLANES = 8
VMEM_LIMIT = 56 * 1024 * 1024

TOKEN_TILE = 256
NA_GROUP = LANES // NA_HEAD_DIM


def _dot(a, b):
    return jnp.dot(a, b, preferred_element_type=F32)


def _dot_nt(a, b):
    return lax.dot_general(a, b, (((1,), (1,)), ((), ())), preferred_element_type=F32)


def _rms(x, g):
    return x * lax.rsqrt(jnp.mean(x * x, -1, keepdims=True) + NORM_EPS) * g


def _silu(x):
    return x * jax.nn.sigmoid(x)


def _gelu(x):
    return x * (lax.erf(x * np.float32(1.0 / np.sqrt(2.0))) + 1.0) * 0.5


def _params(*sem):
    return pltpu.CompilerParams(dimension_semantics=sem, vmem_limit_bytes=VMEM_LIMIT)


def _const_spec(shape):
    zeros = (0,) * len(shape)
    return pl.BlockSpec(shape, lambda *_: zeros)


def _ada_body(c_ref, w_ref, b_ref, o_ref):
    s = _silu(c_ref[...]).astype(BF16)
    o_ref[...] = _dot(s, w_ref[...].astype(BF16)) + b_ref[...]


def _ada(c_all, w_ada, b_ada):
    rows, d = c_all.shape
    width = w_ada.shape[1]
    return pl.pallas_call(
        _ada_body,
        grid=(width // d,),
        in_specs=[pl.BlockSpec((rows, d), lambda j: (0, 0)),
                  pl.BlockSpec((d, d), lambda j: (0, j)),
                  pl.BlockSpec((1, d), lambda j: (0, j))],
        out_specs=pl.BlockSpec((rows, d), lambda j: (0, j)),
        out_shape=jax.ShapeDtypeStruct((rows, width), F32),
        compiler_params=_params("arbitrary"),
        name="ada_mod",
    )(c_all, w_ada, b_ada.reshape(1, width))


def _inproj_body(x_ref, mod_ref, g_ref, wuv_ref, buv_ref, lng_ref, lnb_ref,
                 wt_ref, bt_ref, cos_ref, sin_ref, wna_ref, bna_ref, wg_ref, bg_ref,
                 u_ref, vn_ref, tq_ref, tk_ref, tv_ref, nq_ref, nk_ref, nv_ref,
                 ga_ref, gb_ref):
    y = _rms(x_ref[...], g_ref[...])
    h = (y * (1.0 + mod_ref[0, 1:2, :]) + mod_ref[0, 0:1, :]).astype(BF16)

    puv = _dot(h, wuv_ref[...]) + buv_ref[...]
    u_ref[...] = _gelu(puv[:, :SGU_WIDTH]).astype(BF16)
    v = _gelu(puv[:, SGU_WIDTH:])
    mu = jnp.mean(v, -1, keepdims=True)
    vc = v - mu
    var = jnp.mean(vc * vc, -1, keepdims=True)
    vn_ref[...] = (vc * lax.rsqrt(var + NORM_EPS) * lng_ref[...] + lnb_ref[...]).astype(BF16)

    pt = _dot(h, wt_ref[...]) + bt_ref[...]
    cos = cos_ref[...]
    sin = sin_ref[...]
    tq = pt[:, 0:LANES] * cos + pt[:, LANES:2 * LANES] * sin
    tq_ref[...] = (tq * np.float32(TINY_DIM ** -0.5)).astype(BF16)
    tk_ref[...] = (pt[:, 2 * LANES:3 * LANES] * cos + pt[:, 3 * LANES:4 * LANES] * sin).astype(BF16)
    tv_ref[...] = pt[:, 4 * LANES:5 * LANES].astype(BF16)

    pn = _dot(h, wna_ref[...]) + bna_ref[...]
    nq_ref[...] = (pn[:, :NA_WIDTH] * np.float32(NA_HEAD_DIM ** -0.5)).astype(BF16)
    nk_ref[...] = pn[:, NA_WIDTH:2 * NA_WIDTH].astype(BF16)
    nv_ref[...] = pn[:, 2 * NA_WIDTH:].astype(BF16)

    pg = _dot(h, wg_ref[...]) + bg_ref[...]
    ga_ref[...] = jax.nn.sigmoid(pg[:, :D_MODEL]).astype(BF16)
    gb_ref[...] = jax.nn.sigmoid(pg[:, D_MODEL:]).astype(BF16)


def _inproj(x2, mod3, g_mix, wuv, buv, lng, lnb, wt, bt, cos, sin, wna, bna, wg, bg, seq):
    n, d = x2.shape
    tm = TOKEN_TILE
    tpb = seq // tm
    tok = lambda w: pl.BlockSpec((tm, w), lambda i: (i, 0))
    pos = pl.BlockSpec((tm, LANES), lambda i: (i % tpb, 0))
    in_specs = [tok(d),
                pl.BlockSpec((1, 6, d), lambda i: (i // tpb, 0, 0)),
                _const_spec((1, d)),
                _const_spec(wuv.shape), _const_spec(buv.shape),
                _const_spec((1, d)), _const_spec((1, d)),
                _const_spec(wt.shape), _const_spec(bt.shape), pos, pos,
                _const_spec(wna.shape), _const_spec(bna.shape),
                _const_spec(wg.shape), _const_spec(bg.shape)]
    widths = [SGU_WIDTH, SGU_WIDTH, LANES, LANES, LANES, NA_WIDTH, NA_WIDTH, NA_WIDTH,
              D_MODEL, D_MODEL]
    return pl.pallas_call(
        _inproj_body,
        grid=(n // tm,),
        in_specs=in_specs,
        out_specs=[tok(w) for w in widths],
        out_shape=[jax.ShapeDtypeStruct((n, w), BF16) for w in widths],
        compiler_params=_params("arbitrary"),
        name="in_proj",
    )(x2, mod3, g_mix, wuv, buv, lng, lnb, wt, bt, cos, sin, wna, bna, wg, bg)


def _ctxproj_body(x_ref, mod_ref, g_ref, wt_ref, bt_ref, wna_ref, bna_ref,
                  tk_ref, tv_ref, nk_ref, nv_ref):
    y = _rms(x_ref[...], g_ref[...])
    h = (y * (1.0 + mod_ref[1:2, :]) + mod_ref[0:1, :]).astype(BF16)
    pt = _dot(h, wt_ref[...]) + bt_ref[...]
    tk_ref[...] = pt[:, :LANES].astype(BF16)
    tv_ref[...] = pt[:, LANES:].astype(BF16)
    pn = _dot(h, wna_ref[...]) + bna_ref[...]
    nk_ref[...] = pn[:, :NA_WIDTH].astype(BF16)
    nv_ref[...] = pn[:, NA_WIDTH:].astype(BF16)


def _ctxproj(ctx2, mod_ctx, g_mix, wt, bt, wna, bna):
    n, d = ctx2.shape
    tm = CTX_LEN
    tok = lambda w: pl.BlockSpec((tm, w), lambda i: (i, 0))
    widths = [LANES, LANES, NA_WIDTH, NA_WIDTH]
    return pl.pallas_call(
        _ctxproj_body,
        grid=(n // tm,),
        in_specs=[tok(d), _const_spec((2, d)), _const_spec((1, d)),
                  _const_spec(wt.shape), _const_spec(bt.shape),
                  _const_spec(wna.shape), _const_spec(bna.shape)],
        out_specs=[tok(w) for w in widths],
        out_shape=[jax.ShapeDtypeStruct((n, w), BF16) for w in widths],
        compiler_params=_params("arbitrary"),
        name="ctx_proj",
    )(ctx2, mod_ctx, g_mix, wt, bt, wna, bna)


def _mixer_a_body(tq_ref, tk_ref, tv_ref, tkc_ref, tvc_ref, vn_ref, u_ref,
                  ws_ref, bst_ref, two_ref, oa_ref):
    q = tq_ref[0]
    s_lat = _dot_nt(q, tk_ref[0])
    s_ctx = _dot_nt(q, tkc_ref[0])
    m = jnp.maximum(jnp.max(s_lat, -1, keepdims=True), jnp.max(s_ctx, -1, keepdims=True))
    p_lat = jnp.exp(s_lat - m)
    p_ctx = jnp.exp(s_ctx - m)
    denom = jnp.sum(p_lat, -1, keepdims=True) + jnp.sum(p_ctx, -1, keepdims=True)
    att = _dot(p_lat.astype(BF16), tv_ref[0]) + _dot(p_ctx.astype(BF16), tvc_ref[0])
    att = (att / denom).astype(BF16)
    tiny = _dot(att, two_ref[...])

    for g in range(SGU_GROUPS):
        cols = slice(g * SGU_GROUP_DIM, (g + 1) * SGU_GROUP_DIM)
        mixed = _dot(ws_ref[g], vn_ref[0, :, cols]) + bst_ref[:, g:g + 1]
        gate = mixed + tiny[:, cols]
        oa_ref[0, :, cols] = (u_ref[0, :, cols].astype(F32) * gate).astype(BF16)


def _mixer_a(tq, tk, tv, tkc, tvc, vn, u, ws, bst, two):
    b, s, _ = tq.shape
    blk = lambda w: pl.BlockSpec((1, CHUNK, w), lambda i, j: (i, j, 0))
    per_b = lambda rows, w: pl.BlockSpec((1, rows, w), lambda i, j: (i, 0, 0))
    return pl.pallas_call(
        _mixer_a_body,
        grid=(b, s // CHUNK),
        in_specs=[blk(LANES), per_b(s, LANES), per_b(s, LANES),
                  per_b(CTX_LEN, LANES), per_b(CTX_LEN, LANES),
                  blk(SGU_WIDTH), blk(SGU_WIDTH),
                  _const_spec(ws.shape), _const_spec(bst.shape), _const_spec(two.shape)],
        out_specs=blk(SGU_WIDTH),
        out_shape=jax.ShapeDtypeStruct((b, s, SGU_WIDTH), BF16),
        compiler_params=_params("arbitrary", "arbitrary"),
        name="mixer_a",
    )(tq, tk, tv, tkc, tvc, vn, u, ws, bst, two)


def _na_body(q_ref, k_ref, v_ref, kc_ref, vc_ref, bias_ref, o_ref, *, rows):
    r = pl.program_id(1)
    r_start = jnp.clip(r - NA_ROWS // 2, 0, rows - NA_ROWS)
    k0 = pl.multiple_of(r_start * GRID_W, GRID_W)
    n_lat = NA_ROWS * GRID_W
    stacked = (NA_GROUP * GRID_W, LANES)
    own_lanes = (lax.broadcasted_iota(jnp.int32, stacked, 0) // GRID_W
                 == lax.broadcasted_iota(jnp.int32, stacked, 1) // NA_HEAD_DIM)
    for g in range(NA_WIDTH // LANES):
        cols = slice(g * LANES, (g + 1) * LANES)
        qg = q_ref[0, :, cols]
        qm = jnp.concatenate([qg] * NA_GROUP, 0)
        qm = jnp.where(own_lanes, qm, jnp.zeros_like(qm))
        bias = bias_ref[0, g * NA_GROUP:(g + 1) * NA_GROUP].reshape(stacked[0], n_lat)
        s_lat = _dot_nt(qm, k_ref[0, pl.ds(k0, n_lat), cols]) + bias
        s_ctx = _dot_nt(qm, kc_ref[0, :, cols])
        m = jnp.maximum(jnp.max(s_lat, -1, keepdims=True), jnp.max(s_ctx, -1, keepdims=True))
        p_lat = jnp.exp(s_lat - m)
        p_ctx = jnp.exp(s_ctx - m)
        denom = jnp.sum(p_lat, -1, keepdims=True) + jnp.sum(p_ctx, -1, keepdims=True)
        o = (_dot(p_lat.astype(BF16), v_ref[0, pl.ds(k0, n_lat), cols])
             + _dot(p_ctx.astype(BF16), vc_ref[0, :, cols]))
        o = jnp.where(own_lanes, o / denom, 0.0)
        acc = o[0:GRID_W]
        for j in range(1, NA_GROUP):
            acc = acc + o[j * GRID_W:(j + 1) * GRID_W]
        o_ref[0, :, cols] = acc.astype(BF16)


def _na(nq, nk, nv, nkc, nvc, bias):
    b, s, w = nq.shape
    rows = s // GRID_W
    kh = NA_ROWS

    def bias_idx(i, r):
        return (r - jnp.clip(r - kh // 2, 0, rows - kh), 0, 0, 0)

    per_b = lambda n: pl.BlockSpec((1, n, w), lambda i, r: (i, 0, 0))
    row = pl.BlockSpec((1, GRID_W, w), lambda i, r: (i, r, 0))
    return pl.pallas_call(
        functools.partial(_na_body, rows=rows),
        grid=(b, rows),
        in_specs=[row, per_b(s), per_b(s), per_b(CTX_LEN), per_b(CTX_LEN),
                  pl.BlockSpec((1, NA_HEADS, GRID_W, kh * GRID_W), bias_idx)],
        out_specs=row,
        out_shape=jax.ShapeDtypeStruct((b, s, w), BF16),
        compiler_params=_params("arbitrary", "arbitrary"),
        name="na_attn",
    )(nq, nk, nv, nkc, nvc, bias)


def _na_bias_table(rpb):
    cq = np.arange(GRID_W)
    c_start = np.clip(cq - NA_COLS // 2, 0, GRID_W - NA_COLS)
    col_mask = (cq[None, :] >= c_start[:, None]) & (cq[None, :] < c_start[:, None] + NA_COLS)
    dc_idx = np.clip(cq[None, :] - cq[:, None], -(NA_COLS - 1), NA_COLS - 1) + NA_COLS - 1
    col_bias = rpb[:, :, dc_idx]
    col_bias = jnp.where(col_mask[None, None], col_bias, -jnp.inf)
    off = np.arange(NA_ROWS)[:, None]
    dr = np.arange(NA_ROWS)[None, :] - off + NA_ROWS - 1
    tab = col_bias[:, dr]
    tab = tab.transpose(1, 0, 3, 2, 4)
    return tab.reshape(NA_ROWS, NA_HEADS, GRID_W, NA_ROWS * GRID_W).astype(F32)


def _merge_body(oa_ref, ob_ref, ga_ref, gb_ref, x_ref, mod_ref, wua_ref, wub_ref, wo_ref,
                gf_ref, wr_ref, br_ref, x1_ref, h2_ref, gate_ref, slot_ref):
    a = _dot(oa_ref[...], wua_ref[...])
    b = _dot(ob_ref[...], wub_ref[...])
    merged = ga_ref[...].astype(F32) * a + gb_ref[...].astype(F32) * b
    x1 = x_ref[...] + mod_ref[0, 2:3, :] * _dot(merged.astype(BF16), wo_ref[...])
    x1_ref[...] = x1
    h2 = _rms(x1, gf_ref[...]) * (1.0 + mod_ref[0, 4:5, :]) + mod_ref[0, 3:4, :]
    h2b = h2.astype(BF16)
    h2_ref[...] = h2b

    tm = h2b.shape[0]
    scores = jax.nn.sigmoid(_dot_nt(wr_ref[...], h2b))
    expert = lax.broadcasted_iota(jnp.int32, scores.shape, 0)
    sel = scores + br_ref[...]
    picked = jnp.zeros_like(scores)
    for _ in range(TOP_K):
        best = jnp.max(sel, 0, keepdims=True)
        first = jnp.min(jnp.where(sel == best, expert, N_EXPERTS), 0, keepdims=True)
        hit = expert == first
        picked = jnp.where(hit, scores, picked)
        sel = jnp.where(hit, -jnp.inf, sel)
    gate = picked / jnp.sum(picked, 0, keepdims=True) * np.float32(ROUTE_SCALE)

    routed = picked > 0.0
    earlier = (lax.broadcasted_iota(jnp.int32, (tm, tm), 0)
               < lax.broadcasted_iota(jnp.int32, (tm, tm), 1))
    slot = _dot(jnp.where(routed, 1.0, 0.0).astype(BF16), earlier.astype(BF16))
    slot = jnp.where(routed, slot, -1.0)
    pad = LANES - N_EXPERTS
    gate_ref[...] = jnp.concatenate([gate, jnp.zeros((pad, tm), F32)], 0).T
    slot_ref[...] = jnp.concatenate([slot, jnp.full((pad, tm), -1.0, F32)], 0).T


def _merge(oa, ob, ga, gb, x2, mod3, wua, wub, wo, g_ffn_norm, wr, br, seq):
    n, d = x2.shape
    tm = TOKEN_TILE
    tpb = seq // tm
    tok = lambda w: pl.BlockSpec((tm, w), lambda i: (i, 0))
    return pl.pallas_call(
        _merge_body,
        grid=(n // tm,),
        in_specs=[tok(SGU_WIDTH), tok(NA_WIDTH), tok(d), tok(d), tok(d),
                  pl.BlockSpec((1, 6, d), lambda i: (i // tpb, 0, 0)),
                  _const_spec(wua.shape), _const_spec(wub.shape), _const_spec(wo.shape),
                  _const_spec((1, d)), _const_spec(wr.shape), _const_spec(br.shape)],
        out_specs=[tok(d), tok(d), tok(LANES), tok(LANES)],
        out_shape=[jax.ShapeDtypeStruct((n, d), F32),
                   jax.ShapeDtypeStruct((n, d), BF16),
                   jax.ShapeDtypeStruct((n, LANES), F32),
                   jax.ShapeDtypeStruct((n, LANES), F32)],
        compiler_params=_params("arbitrary"),
        name="merge_route",
    )(oa, ob, ga, gb, x2, mod3, wua, wub, wo, g_ffn_norm, wr, br)


SLAB = 16
EXPERT_BLOCK = 1024
LOCAL_ROWS = 3072
ROW_CHUNK = 1024
assert LOCAL_ROWS >= TOKEN_TILE * TOP_K + N_EXPERTS * (SLAB - 1) and LOCAL_ROWS % ROW_CHUNK == 0
LOCAL_SLABS = LOCAL_ROWS // SLAB
TAB_COUNT = LOCAL_SLABS
TAB_WIDTH = LOCAL_SLABS + LANES // 2
WAIT_SLABS = 8


def _segment_tables(slot, n_tiles):
    routed = (slot[:, :N_EXPERTS] >= 0).reshape(n_tiles, TOKEN_TILE, N_EXPERTS)
    cnt = jnp.sum(routed.astype(jnp.int32), 1)
    size = (cnt + SLAB - 1) // SLAB
    end = jnp.cumsum(size, 1)
    off = end - size
    total = jnp.sum(size, 0)
    per_block = EXPERT_BLOCK // SLAB
    blocks = (total + per_block - 1) // per_block
    block_end = jnp.cumsum(blocks)
    block_start = block_end - blocks
    base = block_start[None, :] * per_block + jnp.cumsum(size, 0) - size

    q = jnp.arange(LOCAL_SLABS)
    owner = jnp.sum((end[:, None, :] <= q[None, :, None]).astype(jnp.int32), -1)
    shift = jnp.take_along_axis(base - off, jnp.minimum(owner, N_EXPERTS - 1), 1)
    table = jnp.concatenate(
        [shift + q[None, :], end[:, -1:], jnp.zeros((n_tiles, TAB_WIDTH - LOCAL_SLABS - 1),
                                                    jnp.int32)], 1).astype(jnp.int32)[:, None, :]

    n_blocks = -(-(n_tiles * LOCAL_SLABS) // per_block) + N_EXPERTS
    n_used = block_end[-1]
    b = jnp.minimum(jnp.arange(n_blocks), n_used - 1)
    expert = jnp.minimum(jnp.sum((block_end[None, :] <= b[:, None]).astype(jnp.int32), -1),
                         N_EXPERTS - 1)
    valid = jnp.clip(total[expert] * SLAB - (b - block_start[expert]) * EXPERT_BLOCK,
                     0, EXPERT_BLOCK)
    return (table, b.astype(jnp.int32), expert.astype(jnp.int32), valid.astype(jnp.int32),
            n_used.astype(jnp.int32).reshape(1), n_blocks)


def _for_each_slab(tab_ref, fn):
    def per_slab(q, c):
        fn(pl.multiple_of(q * SLAB, SLAB), pl.multiple_of(tab_ref[0, 0, q] * SLAB, SLAB))
        return c
    lax.fori_loop(0, tab_ref[0, 0, TAB_COUNT], per_slab, 0)


def _drain_slabs(n, wait_many, wait_one):
    shift = WAIT_SLABS.bit_length() - 1
    lax.fori_loop(0, lax.shift_right_logical(n, shift), lambda q, c: (wait_many(), c)[1], 0)
    lax.fori_loop(0, n & (WAIT_SLABS - 1), lambda q, c: (wait_one(), c)[1], 0)


def _slab_sizes(count):
    return jnp.floor((count + (SLAB - 1.0)) * (1.0 / SLAB)) * SLAB


def _dispatch_body(tab_ref, slot_ref, h_ref, xs_ref, xbuf_ref, sem_ref, pending_ref):
    j = pl.program_id(0)
    cur = j % 2
    slot = slot_ref[...]
    count = jnp.sum((slot >= 0.0).astype(F32), 0, keepdims=True)
    size = _slab_sizes(count)
    earlier = (lax.broadcasted_iota(jnp.int32, (LANES, LANES), 0)
               < lax.broadcasted_iota(jnp.int32, (LANES, LANES), 1))
    off = _dot(jnp.broadcast_to(size, (SUBLANES, LANES)).astype(BF16),
               earlier.astype(BF16))[0:1]
    slot_t = slot.T.astype(BF16)
    h = h_ref[...]
    for c in range(LOCAL_ROWS // ROW_CHUNK):
        r = (lax.broadcasted_iota(jnp.int32, (ROW_CHUNK, LANES), 0) + c * ROW_CHUNK).astype(F32)
        seg = (r >= off) & (r < off + size)
        pos = jnp.sum(jnp.where(seg, r - off + 1.0, 0.0), 1, keepdims=True) - 1.0
        want = _dot(jnp.where(seg, 1.0, 0.0).astype(BF16), slot_t)
        onehot = jnp.where(want == pos, 1.0, 0.0).astype(BF16)
        xbuf_ref[cur, c * ROW_CHUNK:(c + 1) * ROW_CHUNK, :] = _dot(onehot, h).astype(BF16)

    def slab_copy(buf, local_row, global_row, rows=SLAB):
        return pltpu.make_async_copy(xbuf_ref.at[buf, pl.ds(local_row, rows), :],
                                     xs_ref.at[pl.ds(global_row, rows), :], sem_ref.at[buf])

    _for_each_slab(tab_ref, lambda lr, gr: slab_copy(cur, lr, gr).start())

    def drain(buf, n):
        _drain_slabs(n, lambda: slab_copy(buf, 0, 0, WAIT_SLABS * SLAB).wait(),
                     lambda: slab_copy(buf, 0, 0).wait())

    @pl.when(j > 0)
    def _():
        drain(1 - cur, pending_ref[0])

    pending_ref[0] = tab_ref[0, 0, TAB_COUNT]

    @pl.when(j == pl.num_programs(0) - 1)
    def _():
        drain(cur, pending_ref[0])


def _dispatch(table, slot, h2, n_rows):
    n, d = h2.shape
    tm = TOKEN_TILE
    return pl.pallas_call(
        _dispatch_body,
        grid=(n // tm,),
        in_specs=[pl.BlockSpec((1, 1, TAB_WIDTH), lambda i: (i, 0, 0),
                               memory_space=pltpu.SMEM),
                  pl.BlockSpec((tm, LANES), lambda i: (i, 0)),
                  pl.BlockSpec((tm, d), lambda i: (i, 0))],
        out_specs=pl.BlockSpec(memory_space=pl.ANY),
        out_shape=jax.ShapeDtypeStruct((n_rows, d), BF16),
        scratch_shapes=[pltpu.VMEM((2, LOCAL_ROWS, d), BF16),
                        pltpu.SemaphoreType.DMA((2,)),
                        pltpu.SMEM((1,), jnp.int32)],
        compiler_params=_params("arbitrary"),
        name="moe_dispatch",
    )(table, slot, h2)


def _expert_body(blk_ref, exp_ref, valid_ref, used_ref, x_ref, wgu_ref, wd_ref, y_ref):
    b = pl.program_id(0)

    @pl.when(b < used_ref[0])
    def _():
        row = lax.broadcasted_iota(jnp.int32, x_ref.shape, 0)
        x = jnp.where(row < valid_ref[b], x_ref[...], jnp.zeros_like(x_ref))
        a = _dot(x, wgu_ref[0])
        act = _silu(a[:, :EXPERT_FF]) * a[:, EXPERT_FF:]
        y_ref[...] = _dot(act.astype(BF16), wd_ref[0]).astype(BF16)


def _experts(xs, block, expert, valid, n_used, n_blocks, wgu, wd):
    _, d = xs.shape
    rows = pl.BlockSpec((EXPERT_BLOCK, d), lambda b, blk, exp, val, used: (blk[b], 0))
    grid_spec = pltpu.PrefetchScalarGridSpec(
        num_scalar_prefetch=4,
        grid=(n_blocks,),
        in_specs=[rows,
                  pl.BlockSpec((1, d, 2 * EXPERT_FF), lambda b, blk, exp, val, used: (exp[b], 0, 0)),
                  pl.BlockSpec((1, EXPERT_FF, d), lambda b, blk, exp, val, used: (exp[b], 0, 0))],
        out_specs=rows)
    return pl.pallas_call(
        _expert_body,
        grid_spec=grid_spec,
        out_shape=jax.ShapeDtypeStruct(xs.shape, BF16),
        compiler_params=_params("arbitrary"),
        name="moe_experts",
    )(block, expert, valid, n_used, xs, wgu, wd)


def _combine_body(tab_ref, nxt_ref, gate_ref, slot_ref, h_ref, x1_ref, mod_ref, wsgu_ref,
                  wsd_ref, fg_ref, ys_ref, o_ref, pw_ref, ybuf_ref, sem_ref):
    j = pl.program_id(0)
    cur = j % 2

    def slab_copy(buf, local_row, global_row, rows=SLAB):
        return pltpu.make_async_copy(ys_ref.at[pl.ds(global_row, rows), :],
                                     ybuf_ref.at[buf, pl.ds(local_row, rows), :], sem_ref.at[buf])

    @pl.when(j == 0)
    def _():
        ybuf_ref[...] = jnp.zeros_like(ybuf_ref)
        _for_each_slab(tab_ref, lambda lr, gr: slab_copy(0, lr, gr).start())

    @pl.when(j < pl.num_programs(0) - 1)
    def _():
        _for_each_slab(nxt_ref, lambda lr, gr: slab_copy(1 - cur, lr, gr).start())

    slot = slot_ref[...]
    slot_t = slot.T
    count = jnp.sum((slot_t >= 0.0).astype(F32), 1, keepdims=True)
    size = _slab_sizes(count)
    earlier = (lax.broadcasted_iota(jnp.int32, (LANES, LANES), 1)
               < lax.broadcasted_iota(jnp.int32, (LANES, LANES), 0))
    off = _dot(earlier.astype(BF16),
               jnp.broadcast_to(size, (LANES, LANES)).astype(BF16))[:, 0:1]
    slot_b = slot.astype(BF16)
    gate_b = gate_ref[...].astype(BF16)
    for c in range(LOCAL_ROWS // ROW_CHUNK):
        r = (lax.broadcasted_iota(jnp.int32, (LANES, ROW_CHUNK), 1) + c * ROW_CHUNK).astype(F32)
        seg = (r >= off) & (r < off + size)
        pos = jnp.sum(jnp.where(seg, r - off + 1.0, 0.0), 0, keepdims=True) - 1.0
        seg_b = jnp.where(seg, 1.0, 0.0).astype(BF16)
        want = _dot(slot_b, seg_b)
        weight = _dot(gate_b, seg_b)
        pw_ref[:, c * ROW_CHUNK:(c + 1) * ROW_CHUNK] = jnp.where(want == pos, weight,
                                                                 0.0).astype(BF16)

    a = _dot(h_ref[...], wsgu_ref[...])
    act = _silu(a[:, :SHARED_FF]) * a[:, SHARED_FF:]
    shared = _dot(act.astype(BF16), wsd_ref[...])

    _drain_slabs(tab_ref[0, 0, TAB_COUNT],
                 lambda: slab_copy(cur, 0, 0, WAIT_SLABS * SLAB).wait(),
                 lambda: slab_copy(cur, 0, 0).wait())
    routed = _dot(pw_ref[...], ybuf_ref[cur])
    y = x1_ref[...] + mod_ref[0, 5:6, :] * (shared + routed)
    o_ref[...] = _rms(y, fg_ref[...])


def _combine(table, ys, gate, slot, h2, x1, mod3, wsgu, wsd, final_g, seq):
    n, d = h2.shape
    tm = TOKEN_TILE
    tpb = seq // tm
    last = n // tm - 1
    tok = lambda w: pl.BlockSpec((tm, w), lambda i: (i, 0))
    tab = lambda f: pl.BlockSpec((1, 1, TAB_WIDTH), lambda i: (f(i), 0, 0),
                                 memory_space=pltpu.SMEM)
    return pl.pallas_call(
        _combine_body,
        grid=(n // tm,),
        in_specs=[tab(lambda i: i), tab(lambda i: jnp.minimum(i + 1, last)),
                  tok(LANES), tok(LANES), tok(d), tok(d),
                  pl.BlockSpec((1, 6, d), lambda i: (i // tpb, 0, 0)),
                  _const_spec(wsgu.shape), _const_spec(wsd.shape), _const_spec((1, d)),
                  pl.BlockSpec(memory_space=pl.ANY)],
        out_specs=tok(d),
        out_shape=jax.ShapeDtypeStruct((n, d), F32),
        scratch_shapes=[pltpu.VMEM((tm, LOCAL_ROWS), BF16),
                        pltpu.VMEM((2, LOCAL_ROWS, d), BF16),
                        pltpu.SemaphoreType.DMA((2,))],
        compiler_params=_params("arbitrary"),
        name="moe_combine",
    )(table, table, gate, slot, h2, x1, mod3, wsgu, wsd, final_g, ys)


def _moe(h2, gate, slot, x1, mod3, wgu, wd, wsgu, wsd, final_g, seq):
    n, _ = h2.shape
    table, block, expert, valid, n_used, n_blocks = _segment_tables(slot, n // TOKEN_TILE)
    xs = _dispatch(table, slot, h2, n_blocks * EXPERT_BLOCK)
    ys = _experts(xs, block, expert, valid, n_used, n_blocks, wgu, wd)
    return _combine(table, ys, gate, slot, h2, x1, mod3, wsgu, wsd, final_g, seq)


def _pad_cols(w, width):
    return jnp.pad(w, ((0, 0), (0, width - w.shape[1])))


def _rope_tables(seq):
    quarter = TINY_DIM // 4
    t = jnp.arange(seq)
    pos = jnp.stack([t // GRID_W, t % GRID_W], -1).astype(F32)
    inv_freq = ROPE_BASE ** (-jnp.arange(quarter, dtype=F32) / quarter)
    ang = pos[:, :, None] * inv_freq
    cos, sin = jnp.cos(ang), jnp.sin(ang)
    cos_t = jnp.stack([cos, cos], 2).reshape(seq, TINY_DIM)
    sin_t = jnp.stack([-sin, sin], 2).reshape(seq, TINY_DIM)
    return _pad_cols(cos_t, LANES), _pad_cols(sin_t, LANES)


def _half_swap_perm():
    quarter = TINY_DIM // 4
    idx = np.arange(TINY_DIM).reshape(2, 2, quarter)
    return idx[:, ::-1, :].reshape(TINY_DIM)


def _layer(x, c, ctx, c_ctx, w_ada, b_ada, norm_mix_g, norm_ffn_g, w_in, b_in,
           sgu_ln_g, sgu_ln_b, sgu_w_s, sgu_b_s, tiny_w_o, na_rpb, w_up_a, w_up_b, w_out,
           w_router, b_router, w_e_gate, w_e_up, w_e_down, w_sh_gate, w_sh_up, w_sh_down,
           final_g):
    b, s, d = x.shape
    n = b * s
    row = lambda v: v.reshape(1, -1)

    n_mod = -(-(b + 1) // SUBLANES) * SUBLANES
    c_all = jnp.concatenate([c, c_ctx[None], jnp.zeros((n_mod - b - 1, d), F32)], 0)
    mod = _ada(c_all, w_ada, b_ada)
    mod3 = mod[:b].reshape(b, 6, d)
    mod_ctx = mod[b, :2 * d].reshape(2, d)

    perm = _half_swap_perm()
    wcols = lambda lo, hi: w_in[:, lo:hi]
    bcols = lambda lo, hi: b_in[lo:hi]
    wq, bq = wcols(COL_TQ, COL_TK), bcols(COL_TQ, COL_TK)
    wk, bk = wcols(COL_TK, COL_TV), bcols(COL_TK, COL_TV)
    wv, bv = wcols(COL_TV, COL_NQ), bcols(COL_TV, COL_NQ)
    padw = lambda w: _pad_cols(w, LANES)
    padb = lambda v: jnp.pad(v, (0, LANES - v.shape[0]))
    wt = jnp.concatenate([padw(wq), padw(wq[:, perm]), padw(wk), padw(wk[:, perm]), padw(wv)],
                         1).astype(BF16)
    bt = row(jnp.concatenate([padb(bq), padb(bq[perm]), padb(bk), padb(bk[perm]), padb(bv)]))
    wt_ctx = jnp.concatenate([padw(wk), padw(wv)], 1).astype(BF16)
    bt_ctx = row(jnp.concatenate([padb(bk), padb(bv)]))
    wuv, buv = wcols(COL_U, COL_TQ).astype(BF16), row(bcols(COL_U, COL_TQ))
    wna, bna = wcols(COL_NQ, COL_GA).astype(BF16), row(bcols(COL_NQ, COL_GA))
    wna_ctx, bna_ctx = wcols(COL_NK, COL_GA).astype(BF16), row(bcols(COL_NK, COL_GA))
    wg, bg = wcols(COL_GA, IN_WIDTH).astype(BF16), row(bcols(COL_GA, IN_WIDTH))
    cos, sin = _rope_tables(s)

    x2 = x.reshape(n, d)
    u, vn, tq, tk, tv, nq, nk, nv, ga, gb = _inproj(
        x2, mod3, row(norm_mix_g), wuv, buv, row(sgu_ln_g), row(sgu_ln_b), wt, bt, cos, sin,
        wna, bna, wg, bg, s)
    tkc, tvc, nkc, nvc = _ctxproj(ctx.reshape(b * CTX_LEN, d), mod_ctx, row(norm_mix_g),
                                  wt_ctx, bt_ctx, wna_ctx, bna_ctx)

    per_b = lambda a, rows: a.reshape(b, rows, a.shape[-1])
    two = jnp.pad(tiny_w_o, ((0, LANES - TINY_DIM), (0, 0))).astype(BF16)
    oa = _mixer_a(per_b(tq, s), per_b(tk, s), per_b(tv, s), per_b(tkc, CTX_LEN),
                  per_b(tvc, CTX_LEN), per_b(vn, s), per_b(u, s),
                  sgu_w_s.astype(BF16), sgu_b_s.T, two)
    ob = _na(per_b(nq, s), per_b(nk, s), per_b(nv, s), per_b(nkc, CTX_LEN),
             per_b(nvc, CTX_LEN), _na_bias_table(na_rpb))

    wr = w_router.T.astype(BF16)
    br = b_router.reshape(N_EXPERTS, 1)
    x1, h2, gate, slot = _merge(oa.reshape(n, SGU_WIDTH), ob.reshape(n, NA_WIDTH), ga, gb, x2,
                                mod3, w_up_a.astype(BF16), w_up_b.astype(BF16),
                                w_out.astype(BF16), row(norm_ffn_g), wr, br, s)

    wgu = jnp.concatenate([w_e_gate, w_e_up], -1).astype(BF16)
    wd = w_e_down.astype(BF16)
    wsgu = jnp.concatenate([w_sh_gate, w_sh_up], -1).astype(BF16)
    wsd = w_sh_down.astype(BF16)
    out = _moe(h2, gate, slot, x1, mod3, wgu, wd, wsgu, wsd, row(final_g), s)
    return out.reshape(b, s, d)


def kernel(x, c, ctx, c_ctx, w_ada, b_ada, norm_mix_g, norm_ffn_g, w_in, b_in, sgu_ln_g, sgu_ln_b, sgu_w_s, sgu_b_s, tiny_w_o, na_rpb, w_up_a, w_up_b, w_out, w_router, b_router, w_e_gate, w_e_up, w_e_down, w_sh_gate, w_sh_up, w_sh_down, final_norm_g):
    assert w_ada.shape[0] == 1, "single-layer block"
    return _layer(x, c, ctx, c_ctx, w_ada[0], b_ada[0], norm_mix_g[0], norm_ffn_g[0], w_in[0],
                  b_in[0], sgu_ln_g[0], sgu_ln_b[0], sgu_w_s[0], sgu_b_s[0], tiny_w_o[0],
                  na_rpb[0], w_up_a[0], w_up_b[0], w_out[0], w_router[0], b_router[0],
                  w_e_gate[0], w_e_up[0], w_e_down[0], w_sh_gate[0], w_sh_up[0], w_sh_down[0],
                  final_norm_g)
```

```python
import functools

import numpy as np
import jax
import jax.numpy as jnp
from jax import lax
from jax.experimental import pallas as pl
from jax.experimental.pallas import tpu as pltpu

F32 = jnp.float32
BF16 = jnp.bfloat16

D_MODEL = 1024
CTX_LEN = 256
GRID_W = 64
NORM_EPS = 1e-6
SGU_WIDTH = 1024
SGU_GROUPS = 8
SGU_GROUP_DIM = SGU_WIDTH // SGU_GROUPS
CHUNK = 128
TINY_DIM = 64
ROPE_BASE = 10000.0
NA_HEADS = 16
NA_HEAD_DIM = 32
NA_WIDTH = NA_HEADS * NA_HEAD_DIM
NA_ROWS = 8
NA_COLS = 16
N_EXPERTS = 64
TOP_K = 8
EXPERT_FF = 256
SHARED_FF = 256
ROUTE_SCALE = 2.5
COL_U = 0
COL_V = COL_U + SGU_WIDTH
COL_TQ = COL_V + SGU_WIDTH
COL_TK = COL_TQ + TINY_DIM
COL_TV = COL_TK + TINY_DIM
COL_NQ = COL_TV + TINY_DIM
COL_NK = COL_NQ + NA_WIDTH
COL_NV = COL_NK + NA_WIDTH
COL_GA = COL_NV + NA_WIDTH
COL_GB = COL_GA + D_MODEL
IN_WIDTH = COL_GB + D_MODEL

LANES = 128
SUBLANES = 8
VMEM_LIMIT = 56 * 1024 * 1024

TOKEN_TILE = 256
MIX_CHUNKS = 4
NA_GROUP = LANES // NA_HEAD_DIM
LOG2_E = float(np.log2(np.e))


def _dot(a, b):
    return jnp.dot(a, b, preferred_element_type=F32)


def _dot_nt(a, b):
    return lax.dot_general(a, b, (((1,), (1,)), ((), ())), preferred_element_type=F32)


def _dot_tn(a, b):
    return lax.dot_general(a, b, (((0,), (0,)), ((), ())), preferred_element_type=F32)


def _rms(x, g):
    return x * lax.rsqrt(jnp.mean(x * x, -1, keepdims=True) + NORM_EPS) * g


def _silu(x):
    return x * jax.nn.sigmoid(x)


def _gelu(x):
    return x * (lax.erf(x * np.float32(1.0 / np.sqrt(2.0))) + 1.0) * 0.5


def _params(*sem):
    return pltpu.CompilerParams(dimension_semantics=sem, vmem_limit_bytes=VMEM_LIMIT)


def _const_spec(shape):
    zeros = (0,) * len(shape)
    return pl.BlockSpec(shape, lambda *_: zeros)


def _ada_body(c_ref, w_ref, b_ref, o_ref):
    s = _silu(c_ref[...]).astype(BF16)
    o_ref[...] = _dot(s, w_ref[...].astype(BF16)) + b_ref[...]


def _ada(c_all, w_ada, b_ada):
    rows, d = c_all.shape
    width = w_ada.shape[1]
    return pl.pallas_call(
        _ada_body,
        grid=(width // d,),
        in_specs=[pl.BlockSpec((rows, d), lambda j: (0, 0)),
                  pl.BlockSpec((d, d), lambda j: (0, j)),
                  pl.BlockSpec((1, d), lambda j: (0, j))],
        out_specs=pl.BlockSpec((rows, d), lambda j: (0, j)),
        out_shape=jax.ShapeDtypeStruct((rows, width), F32),
        compiler_params=_params("arbitrary"),
        name="ada_mod",
    )(c_all, w_ada, b_ada.reshape(1, width))


def _inproj_body(x_ref, mod_ref, g_ref, wuv_ref, buv_ref, lng_ref, lnb_ref,
                 wt_ref, bt_ref, cos_ref, sin_ref, wna_ref, bna_ref, wg_ref, bg_ref,
                 u_ref, vn_ref, tq_ref, tk_ref, tv_ref, nq_ref, nk_ref, nv_ref,
                 ga_ref, gb_ref):
    y = _rms(x_ref[...], g_ref[...])
    h = (y * (1.0 + mod_ref[0, 1:2, :]) + mod_ref[0, 0:1, :]).astype(BF16)

    puv =_dot(h, wuv_ref[...]) + buv_ref[...]
    pt = _dot(h, wt_ref[...]) + bt_ref[...]
    pn = _dot(h, wna_ref[...]) + bna_ref[...]
    pg = _dot(h, wg_ref[...]) + bg_ref[...]

    u_ref[...] = _gelu(puv[:, :SGU_WIDTH]).astype(BF16)
    v = _gelu(puv[:, SGU_WIDTH:])
    mu = jnp.mean(v, -1, keepdims=True)
    vc = v - mu
    var = jnp.mean(vc * vc, -1, keepdims=True)
    vn_ref[...] = (vc * lax.rsqrt(var + NORM_EPS) * lng_ref[...] + lnb_ref[...]).astype(BF16)

    cos = cos_ref[...]
    sin = sin_ref[...]
    tq = pt[:, 0:LANES] * cos + pt[:, LANES:2 * LANES] * sin
    tq_ref[...] = (tq * np.float32(TINY_DIM ** -0.5)).astype(BF16)
    tk_ref[...] = (pt[:, 2 * LANES:3 * LANES] * cos + pt[:, 3 * LANES:4 * LANES] * sin).astype(BF16)
    tv_ref[...] = pt[:, 4 * LANES:5 * LANES].astype(BF16)

    nq_ref[...] = (pn[:, :NA_WIDTH] * np.float32(NA_HEAD_DIM ** -0.5 * LOG2_E)).astype(BF16)
    nk_ref[...] = pn[:, NA_WIDTH:2 * NA_WIDTH].astype(BF16)
    nv_ref[...] = pn[:, 2 * NA_WIDTH:].astype(BF16)

    ga_ref[...] = jax.nn.sigmoid(pg[:, :D_MODEL]).astype(BF16)
    gb_ref[...] = jax.nn.sigmoid(pg[:, D_MODEL:]).astype(BF16)


def _inproj(x2, mod3, g_mix, wuv, buv, lng, lnb, wt, bt, cos, sin, wna, bna, wg, bg, seq):
    n, d = x2.shape
    tm = TOKEN_TILE
    tpb = seq // tm
    tok = lambda w: pl.BlockSpec((tm, w), lambda i: (i, 0))
    pos = pl.BlockSpec((tm, LANES), lambda i: (i % tpb, 0))
    in_specs = [tok(d),
                pl.BlockSpec((1, 6, d), lambda i: (i // tpb, 0, 0)),
                _const_spec((1, d)),
                _const_spec(wuv.shape), _const_spec(buv.shape),
                _const_spec((1, d)), _const_spec((1, d)),
                _const_spec(wt.shape), _const_spec(bt.shape), pos, pos,
                _const_spec(wna.shape), _const_spec(bna.shape),
                _const_spec(wg.shape), _const_spec(bg.shape)]
    widths = [SGU_WIDTH, SGU_WIDTH, LANES, LANES, LANES, NA_WIDTH, NA_WIDTH, NA_WIDTH,
              D_MODEL, D_MODEL]
    return pl.pallas_call(
        _inproj_body,
        grid=(n // tm,),
        in_specs=in_specs,
        out_specs=[tok(w) for w in widths],
        out_shape=[jax.ShapeDtypeStruct((n, w), BF16) for w in widths],
        compiler_params=_params("arbitrary"),
        name="in_proj",
    )(x2, mod3, g_mix, wuv, buv, lng, lnb, wt, bt, cos, sin, wna, bna, wg, bg)


def _ctxproj_body(x_ref, mod_ref, g_ref, wt_ref, bt_ref, wna_ref, bna_ref,
                  tk_ref, tv_ref, nk_ref, nv_ref):
    y = _rms(x_ref[...], g_ref[...])
    h = (y * (1.0 + mod_ref[1:2, :]) + mod_ref[0:1, :]).astype(BF16)
    pt = _dot(h, wt_ref[...]) + bt_ref[...]
    tk_ref[...] = pt[:, :LANES].astype(BF16)
    tv_ref[...] = pt[:, LANES:].astype(BF16)
    pn = _dot(h, wna_ref[...]) + bna_ref[...]
    nk_ref[...] = pn[:, :NA_WIDTH].astype(BF16)
    nv_ref[...] = pn[:, NA_WIDTH:].astype(BF16)


def _ctxproj(ctx2, mod_ctx, g_mix, wt, bt, wna, bna):
    n, d = ctx2.shape
    tm = CTX_LEN
    tok = lambda w: pl.BlockSpec((tm, w), lambda i: (i, 0))
    widths = [LANES, LANES, NA_WIDTH, NA_WIDTH]
    return pl.pallas_call(
        _ctxproj_body,
        grid=(n // tm,),
        in_specs=[tok(d), _const_spec((2, d)), _const_spec((1, d)),
                  _const_spec(wt.shape), _const_spec(bt.shape),
                  _const_spec(wna.shape), _const_spec(bna.shape)],
        out_specs=[tok(w) for w in widths],
        out_shape=[jax.ShapeDtypeStruct((n, w), BF16) for w in widths],
        compiler_params=_params("arbitrary"),
        name="ctx_proj",
    )(ctx2, mod_ctx, g_mix, wt, bt, wna, bna)


def _mixer_a_body(tq_ref, tk_ref, tv_ref, tkc_ref, tvc_ref, vn_ref, u_ref,
                  ws_ref, bst_ref, two_ref, oa_ref):
    q = tq_ref[0]
    s_lat = _dot_nt(tk_ref[0], q)
    s_ctx = _dot_nt(tkc_ref[0], q)
    group = lambda g: slice(g * SGU_GROUP_DIM, (g + 1) * SGU_GROUP_DIM)
    chunk = lambda c: slice(c * CHUNK, (c + 1) * CHUNK)
    mixed = [_dot(ws_ref[g], jnp.concatenate([vn_ref[0, chunk(c), group(g)]
                                              for c in range(MIX_CHUNKS)], 1))
             for g in range(SGU_GROUPS)]
    m = jnp.maximum(jnp.max(s_lat, 0, keepdims=True), jnp.max(s_ctx, 0, keepdims=True))
    p_lat = jnp.exp(s_lat - m)
    p_ctx = jnp.exp(s_ctx - m)
    denom = jnp.sum(p_lat, 0, keepdims=True) + jnp.sum(p_ctx, 0, keepdims=True)
    att_t = _dot_tn(tv_ref[0], p_lat.astype(BF16)) + _dot_tn(tvc_ref[0], p_ctx.astype(BF16))
    att = (att_t / denom).T.astype(BF16)
    tiny = _dot(att, two_ref[...])

    for g in range(SGU_GROUPS):
        for c in range(MIX_CHUNKS):
            gate = (mixed[g][:, chunk(c)] + bst_ref[:, g:g + 1] + tiny[chunk(c), group(g)])
            oa_ref[0, chunk(c), group(g)] = (u_ref[0, chunk(c), group(g)].astype(F32)
                                             * gate).astype(BF16)


def _mixer_a(tq, tk, tv, tkc, tvc, vn, u, ws, bst, two):
    b, s, _ = tq.shape
    tile = MIX_CHUNKS * CHUNK
    blk = lambda w: pl.BlockSpec((1, tile, w), lambda i, j: (i, j, 0))
    per_b = lambda rows, w: pl.BlockSpec((1, rows, w), lambda i, j: (i, 0, 0))
    return pl.pallas_call(
        _mixer_a_body,
        grid=(b, s // tile),
        in_specs=[blk(LANES), per_b(s, LANES), per_b(s, LANES),
                  per_b(CTX_LEN, LANES), per_b(CTX_LEN, LANES),
                  blk(SGU_WIDTH), blk(SGU_WIDTH),
                  _const_spec(ws.shape), _const_spec(bst.shape), _const_spec(two.shape)],
        out_specs=blk(SGU_WIDTH),
        out_shape=jax.ShapeDtypeStruct((b, s, SGU_WIDTH), BF16),
        compiler_params=_params("arbitrary", "arbitrary"),
        name="mixer_a",
    )(tq, tk, tv, tkc, tvc, vn, u, ws, bst, two)


def _na_body(q_ref, k_ref, v_ref, kc_ref, vc_ref, bias_ref, o_ref, *, rows):
    r = pl.program_id(1)
    r_start = jnp.clip(r - NA_ROWS // 2, 0, rows - NA_ROWS)
    k0 = pl.multiple_of(r_start * GRID_W, GRID_W)
    n_lat = NA_ROWS * GRID_W
    stacked = (NA_GROUP * GRID_W, LANES)
    own_lanes = (lax.broadcasted_iota(jnp.int32, stacked, 0) // GRID_W
                 == lax.broadcasted_iota(jnp.int32, stacked, 1) // NA_HEAD_DIM)
    n_groups = NA_WIDTH // LANES

    def scores(g):
        cols = slice(g * LANES, (g + 1) * LANES)
        qg = q_ref[0, :, cols]
        qm = jnp.concatenate([qg] * NA_GROUP, 0)
        qm = jnp.where(own_lanes, qm, jnp.zeros_like(qm))
        return (_dot_nt(k_ref[0, pl.ds(k0, n_lat), cols], qm) + bias_ref[0, g],
                _dot_nt(kc_ref[0, :, cols], qm))

    all_scores = [scores(g) for g in range(n_groups)]
    for g in range(n_groups):
        cols = slice(g * LANES, (g + 1) * LANES)
        s_lat, s_ctx = all_scores[g]
        m = jnp.maximum(jnp.max(s_lat, 0, keepdims=True), jnp.max(s_ctx, 0, keepdims=True))
        p_lat = jnp.exp2(s_lat - m)
        p_ctx = jnp.exp2(s_ctx - m)
        denom = jnp.sum(p_lat, 0, keepdims=True) + jnp.sum(p_ctx, 0, keepdims=True)
        o_t = (_dot_tn(v_ref[0, pl.ds(k0, n_lat), cols], p_lat.astype(BF16))
               + _dot_tn(vc_ref[0, :, cols], p_ctx.astype(BF16)))
        o = jnp.where(own_lanes, (o_t / denom).T, 0.0)
        acc = o[0:GRID_W]
        for j in range(1, NA_GROUP):
            acc = acc + o[j * GRID_W:(j + 1) * GRID_W]
        o_ref[0, :, cols] = acc.astype(BF16)


def _na(nq, nk, nv, nkc, nvc, bias):
    b, s, w = nq.shape
    rows = s // GRID_W
    kh = NA_ROWS

    def bias_idx(i, r):
        return (r - jnp.clip(r - kh // 2, 0, rows - kh), 0, 0, 0)

    per_b = lambda n: pl.BlockSpec((1, n, w), lambda i, r: (i, 0, 0))
    row = pl.BlockSpec((1, GRID_W, w), lambda i, r: (i, r, 0))
    return pl.pallas_call(
        functools.partial(_na_body, rows=rows),
        grid=(b, rows),
        in_specs=[row, per_b(s), per_b(s), per_b(CTX_LEN), per_b(CTX_LEN),
                  pl.BlockSpec((1, NA_HEADS // NA_GROUP, kh * GRID_W, NA_GROUP * GRID_W),
                               bias_idx)],
        out_specs=row,
        out_shape=jax.ShapeDtypeStruct((b, s, w), BF16),
        compiler_params=_params("arbitrary", "arbitrary"),
        name="na_attn",
    )(nq, nk, nv, nkc, nvc, bias)


def _na_bias_table(rpb):
    cq = np.arange(GRID_W)
    c_start = np.clip(cq - NA_COLS // 2, 0, GRID_W - NA_COLS)
    col_mask = (cq[None, :] >= c_start[:, None]) & (cq[None, :] < c_start[:, None] + NA_COLS)
    dc_idx = np.clip(cq[None, :] - cq[:, None], -(NA_COLS - 1), NA_COLS - 1) + NA_COLS - 1
    col_bias = rpb[:, :, dc_idx]
    col_bias = jnp.where(col_mask[None, None], col_bias, -jnp.inf)
    off = np.arange(NA_ROWS)[:, None]
    dr = np.arange(NA_ROWS)[None, :] - off + NA_ROWS - 1
    tab = col_bias[:, dr]
    tab = tab.transpose(1, 2, 4, 0, 3)
    tab = tab.reshape(NA_ROWS, NA_ROWS * GRID_W, NA_HEADS // NA_GROUP, NA_GROUP * GRID_W)
    return (tab.transpose(0, 2, 1, 3) * np.float32(LOG2_E)).astype(F32)


def _merge_body(oa_ref, ob_ref, ga_ref, gb_ref, x_ref, mod_ref, wua_ref, wub_ref, wo_ref,
                gf_ref, wr_ref, br_ref, x1_ref, h2_ref, gate_ref, slot_ref):
    a = _dot(oa_ref[...], wua_ref[...])
    b = _dot(ob_ref[...], wub_ref[...])
    merged = ga_ref[...].astype(F32) * a + gb_ref[...].astype(F32) * b
    x1 = x_ref[...] + mod_ref[0, 2:3, :] * _dot(merged.astype(BF16), wo_ref[...])
    x1_ref[...] = x1
    h2 = _rms(x1, gf_ref[...]) * (1.0 + mod_ref[0, 4:5, :]) + mod_ref[0, 3:4, :]
    h2b = h2.astype(BF16)
    h2_ref[...] = h2b

    tm = h2b.shape[0]
    scores = jax.nn.sigmoid(_dot_nt(wr_ref[...], h2b))
    expert = lax.broadcasted_iota(jnp.int32, scores.shape, 0)
    sel = scores + br_ref[...]
    picked = jnp.zeros_like(scores)
    for _ in range(TOP_K):
        best = jnp.max(sel, 0, keepdims=True)
        first = jnp.min(jnp.where(sel == best, expert, N_EXPERTS), 0, keepdims=True)
        hit = expert == first
        picked = jnp.where(hit, scores, picked)
        sel = jnp.where(hit, -jnp.inf, sel)
    gate = picked / jnp.sum(picked, 0, keepdims=True) * np.float32(ROUTE_SCALE)

    routed = picked > 0.0
    earlier = (lax.broadcasted_iota(jnp.int32, (tm, tm), 0)
               < lax.broadcasted_iota(jnp.int32, (tm, tm), 1))
    slot = _dot(jnp.where(routed, 1.0, 0.0).astype(BF16), earlier.astype(BF16))
    slot = jnp.where(routed, slot, -1.0)
    pad = LANES - N_EXPERTS
    gate_ref[...] = jnp.concatenate([gate, jnp.zeros((pad, tm), F32)], 0).T
    slot_ref[...] = jnp.concatenate([slot, jnp.full((pad, tm), -1.0, F32)], 0).T


def _merge(oa, ob, ga, gb, x2, mod3, wua, wub, wo, g_ffn_norm, wr, br, seq):
    n, d = x2.shape
    tm = TOKEN_TILE
    tpb = seq // tm
    tok = lambda w: pl.BlockSpec((tm, w), lambda i: (i, 0))
    return pl.pallas_call(
        _merge_body,
        grid=(n // tm,),
        in_specs=[tok(SGU_WIDTH), tok(NA_WIDTH), tok(d), tok(d), tok(d),
                  pl.BlockSpec((1, 6, d), lambda i: (i // tpb, 0, 0)),
                  _const_spec(wua.shape), _const_spec(wub.shape), _const_spec(wo.shape),
                  _const_spec((1, d)), _const_spec(wr.shape), _const_spec(br.shape)],
        out_specs=[tok(d), tok(d), tok(LANES), tok(LANES)],
        out_shape=[jax.ShapeDtypeStruct((n, d), F32),
                   jax.ShapeDtypeStruct((n, d), BF16),
                   jax.ShapeDtypeStruct((n, LANES), F32),
                   jax.ShapeDtypeStruct((n, LANES), F32)],
        compiler_params=_params("arbitrary"),
        name="merge_route",
    )(oa, ob, ga, gb, x2, mod3, wua, wub, wo, g_ffn_norm, wr, br)


SLAB = 16
EXPERT_BLOCK = 1024
LOCAL_ROWS = 3072
ROW_CHUNK = 1024
assert LOCAL_ROWS >= TOKEN_TILE * TOP_K + N_EXPERTS * (SLAB - 1) and LOCAL_ROWS % ROW_CHUNK == 0
LOCAL_SLABS = LOCAL_ROWS // SLAB
TAB_COUNT = LOCAL_SLABS
TAB_WIDTH = LOCAL_SLABS + LANES // 2
WAIT_SLABS = 8


def _segment_tables(slot, n_tiles):
    routed = (slot[:, :N_EXPERTS] >= 0).reshape(n_tiles, TOKEN_TILE, N_EXPERTS)
    cnt = jnp.sum(routed.astype(jnp.int32), 1)
    size = (cnt + SLAB - 1) // SLAB
    end = jnp.cumsum(size, 1)
    off = end - size
    total = jnp.sum(size, 0)
    per_block = EXPERT_BLOCK // SLAB
    blocks = (total + per_block - 1) // per_block
    block_end = jnp.cumsum(blocks)
    block_start = block_end - blocks
    base = block_start[None, :] * per_block + jnp.cumsum(size, 0) - size

    q = jnp.arange(LOCAL_SLABS)
    owned = (off[:, None, :] <= q[None, :, None]) & (q[None, :, None] < end[:, None, :])
    shift = jnp.sum(jnp.where(owned, (base - off)[:, None, :], 0), -1)
    table = jnp.concatenate(
        [shift + q[None, :], end[:, -1:], jnp.zeros((n_tiles, TAB_WIDTH - LOCAL_SLABS - 1),
                                                    jnp.int32)], 1).astype(jnp.int32)[:, None, :]

    n_blocks = -(-(n_tiles * LOCAL_SLABS) // per_block) + N_EXPERTS
    n_used = block_end[-1]
    b = jnp.minimum(jnp.arange(n_blocks), n_used - 1)
    expert = jnp.minimum(jnp.sum((block_end[None, :] <= b[:, None]).astype(jnp.int32), -1),
                         N_EXPERTS - 1)
    valid = jnp.clip(total[expert] * SLAB - (b - block_start[expert]) * EXPERT_BLOCK,
                     0, EXPERT_BLOCK)
    return (table, b.astype(jnp.int32), expert.astype(jnp.int32), valid.astype(jnp.int32),
            n_used.astype(jnp.int32).reshape(1), n_blocks)


def _for_each_slab(tab_ref, fn):
    def per_slab(q, c):
        fn(pl.multiple_of(q * SLAB, SLAB), pl.multiple_of(tab_ref[0, 0, q] * SLAB, SLAB))
        return c
    lax.fori_loop(0, tab_ref[0, 0, TAB_COUNT], per_slab, 0)


def _drain_slabs(n, wait_many, wait_one):
    shift = WAIT_SLABS.bit_length() - 1
    lax.fori_loop(0, lax.shift_right_logical(n, shift), lambda q, c: (wait_many(), c)[1], 0)
    lax.fori_loop(0, n & (WAIT_SLABS - 1), lambda q, c: (wait_one(), c)[1], 0)


def _slab_sizes(count):
    return jnp.floor((count + (SLAB - 1.0)) * (1.0 / SLAB)) * SLAB


def _dispatch_body(tab_ref, slot_ref, h_ref, xs_ref, xbuf_ref, sem_ref, pending_ref):
    j = pl.program_id(0)
    cur = j % 2
    slot = slot_ref[...]
    count = jnp.sum((slot >= 0.0).astype(F32), 0, keepdims=True)
    size = _slab_sizes(count)
    earlier = (lax.broadcasted_iota(jnp.int32, (LANES, LANES), 0)
               < lax.broadcasted_iota(jnp.int32, (LANES, LANES), 1))
    off = _dot(jnp.broadcast_to(size, (SUBLANES, LANES)).astype(BF16),
               earlier.astype(BF16))[0:1]
    slot_t = slot.T.astype(BF16)
    h = h_ref[...]
    for c in range(LOCAL_ROWS // ROW_CHUNK):
        r = (lax.broadcasted_iota(jnp.int32, (ROW_CHUNK, LANES), 0) + c * ROW_CHUNK).astype(F32)
        seg = (r >= off) & (r < off + size)
        pos = jnp.sum(jnp.where(seg, r - off + 1.0, 0.0), 1, keepdims=True) - 1.0
        want = _dot(jnp.where(seg, 1.0, 0.0).astype(BF16), slot_t)
        onehot = jnp.where(want == pos, 1.0, 0.0).astype(BF16)
        xbuf_ref[cur, c * ROW_CHUNK:(c + 1) * ROW_CHUNK, :] = _dot(onehot, h).astype(BF16)

    def slab_copy(buf, local_row, global_row, rows=SLAB):
        return pltpu.make_async_copy(xbuf_ref.at[buf, pl.ds(local_row, rows), :],
                                     xs_ref.at[pl.ds(global_row, rows), :], sem_ref.at[buf])

    _for_each_slab(tab_ref, lambda lr, gr: slab_copy(cur, lr, gr).start())

    def drain(buf, n):
        _drain_slabs(n, lambda: slab_copy(buf, 0, 0, WAIT_SLABS * SLAB).wait(),
                     lambda: slab_copy(buf, 0, 0).wait())

    @pl.when(j > 0)
    def _():
        drain(1 - cur, pending_ref[0])

    pending_ref[0] = tab_ref[0, 0, TAB_COUNT]

    @pl.when(j == pl.num_programs(0) - 1)
    def _():
        drain(cur, pending_ref[0])


def _dispatch(table, slot, h2, n_rows):
    n, d = h2.shape
    tm = TOKEN_TILE
    return pl.pallas_call(
        _dispatch_body,
        grid=(n // tm,),
        in_specs=[pl.BlockSpec((1, 1, TAB_WIDTH), lambda i: (i, 0, 0),
                               memory_space=pltpu.SMEM),
                  pl.BlockSpec((tm, LANES), lambda i: (i, 0)),
                  pl.BlockSpec((tm, d), lambda i: (i, 0))],
        out_specs=pl.BlockSpec(memory_space=pl.ANY),
        out_shape=jax.ShapeDtypeStruct((n_rows, d), BF16),
        scratch_shapes=[pltpu.VMEM((2, LOCAL_ROWS, d), BF16),
                        pltpu.SemaphoreType.DMA((2,)),
                        pltpu.SMEM((1,), jnp.int32)],
        compiler_params=_params("arbitrary"),
        name="moe_dispatch",
    )(table, slot, h2)


def _expert_body(blk_ref, exp_ref, valid_ref, used_ref, x_ref, wgu_ref, wd_ref, y_ref):
    b = pl.program_id(0)

    @pl.when(b < used_ref[0])
    def _():
        row = lax.broadcasted_iota(jnp.int32, x_ref.shape, 0)
        x = jnp.where(row < valid_ref[b], x_ref[...], jnp.zeros_like(x_ref))
        a = _dot(x, wgu_ref[0])
        act = _silu(a[:, :EXPERT_FF]) * a[:, EXPERT_FF:]
        y_ref[...] = _dot(act.astype(BF16), wd_ref[0]).astype(BF16)


def _experts(xs, block, expert, valid, n_used, n_blocks, wgu, wd):
    _, d = xs.shape
    rows = pl.BlockSpec((EXPERT_BLOCK, d), lambda b, blk, exp, val, used: (blk[b], 0))
    grid_spec = pltpu.PrefetchScalarGridSpec(
        num_scalar_prefetch=4,
        grid=(n_blocks,),
        in_specs=[rows,
                  pl.BlockSpec((1, d, 2 * EXPERT_FF), lambda b, blk, exp, val, used: (exp[b], 0, 0)),
                  pl.BlockSpec((1, EXPERT_FF, d), lambda b, blk, exp, val, used: (exp[b], 0, 0))],
        out_specs=rows)
    return pl.pallas_call(
        _expert_body,
        grid_spec=grid_spec,
        out_shape=jax.ShapeDtypeStruct(xs.shape, BF16),
        compiler_params=_params("arbitrary"),
        name="moe_experts",
    )(block, expert, valid, n_used, xs, wgu, wd)


def _combine_body(tab_ref, nxt_ref, gate_ref, slot_ref, h_ref, x1_ref, mod_ref, wsgu_ref,
                  wsd_ref, fg_ref, ys_ref, o_ref, pw_ref, ybuf_ref, sem_ref):
    j = pl.program_id(0)
    cur = j % 2

    def slab_copy(buf, local_row, global_row, rows=SLAB):
        return pltpu.make_async_copy(ys_ref.at[pl.ds(global_row, rows), :],
                                     ybuf_ref.at[buf, pl.ds(local_row, rows), :], sem_ref.at[buf])

    @pl.when(j == 0)
    def _():
        ybuf_ref[...] = jnp.zeros_like(ybuf_ref)
        _for_each_slab(tab_ref, lambda lr, gr: slab_copy(0, lr, gr).start())

    @pl.when(j < pl.num_programs(0) - 1)
    def _():
        _for_each_slab(nxt_ref, lambda lr, gr: slab_copy(1 - cur, lr, gr).start())

    slot = slot_ref[...]
    slot_t = slot.T
    count = jnp.sum((slot_t >= 0.0).astype(F32), 1, keepdims=True)
    size = _slab_sizes(count)
    earlier = (lax.broadcasted_iota(jnp.int32, (LANES, LANES), 1)
               < lax.broadcasted_iota(jnp.int32, (LANES, LANES), 0))
    off = _dot(earlier.astype(BF16),
               jnp.broadcast_to(size, (LANES, LANES)).astype(BF16))[:, 0:1]
    slot_b = slot.astype(BF16)
    gate_b = gate_ref[...].astype(BF16)
    for c in range(LOCAL_ROWS // ROW_CHUNK):
        r = (lax.broadcasted_iota(jnp.int32, (LANES, ROW_CHUNK), 1) + c * ROW_CHUNK).astype(F32)
        seg = (r >= off) & (r < off + size)
        pos = jnp.sum(jnp.where(seg, r - off + 1.0, 0.0), 0, keepdims=True) - 1.0
        seg_b = jnp.where(seg, 1.0, 0.0).astype(BF16)
        want = _dot(slot_b, seg_b)
        weight = _dot(gate_b, seg_b)
        pw_ref[:, c * ROW_CHUNK:(c + 1) * ROW_CHUNK] = jnp.where(want == pos, weight,
                                                                 0.0).astype(BF16)

    a = _dot(h_ref[...], wsgu_ref[...])
    act = _silu(a[:, :SHARED_FF]) * a[:, SHARED_FF:]
    shared = _dot(act.astype(BF16), wsd_ref[...])

    _drain_slabs(tab_ref[0, 0, TAB_COUNT],
                 lambda: slab_copy(cur, 0, 0, WAIT_SLABS * SLAB).wait(),
                 lambda: slab_copy(cur, 0, 0).wait())
    routed = _dot(pw_ref[...], ybuf_ref[cur])
    y = x1_ref[...] + mod_ref[0, 5:6, :] * (shared + routed)
    o_ref[...] = _rms(y, fg_ref[...])


def _combine(table, ys, gate, slot, h2, x1, mod3, wsgu, wsd, final_g, seq):
    n, d = h2.shape
    tm = TOKEN_TILE
    tpb = seq // tm
    last = n // tm - 1
    tok = lambda w: pl.BlockSpec((tm, w), lambda i: (i, 0))
    tab = lambda f: pl.BlockSpec((1, 1, TAB_WIDTH), lambda i: (f(i), 0, 0),
                                 memory_space=pltpu.SMEM)
    return pl.pallas_call(
        _combine_body,
        grid=(n // tm,),
        in_specs=[tab(lambda i: i), tab(lambda i: jnp.minimum(i + 1, last)),
                  tok(LANES), tok(LANES), tok(d), tok(d),
                  pl.BlockSpec((1, 6, d), lambda i: (i // tpb, 0, 0)),
                  _const_spec(wsgu.shape), _const_spec(wsd.shape), _const_spec((1, d)),
                  pl.BlockSpec(memory_space=pl.ANY)],
        out_specs=tok(d),
        out_shape=jax.ShapeDtypeStruct((n, d), F32),
        scratch_shapes=[pltpu.VMEM((tm, LOCAL_ROWS), BF16),
                        pltpu.VMEM((2, LOCAL_ROWS, d), BF16),
                        pltpu.SemaphoreType.DMA((2,))],
        compiler_params=_params("arbitrary"),
        name="moe_combine",
    )(table, table, gate, slot, h2, x1, mod3, wsgu, wsd, final_g, ys)


def _moe(h2, gate, slot, x1, mod3, wgu, wd, wsgu, wsd, final_g, seq):
    n, _ = h2.shape
    table, block, expert, valid, n_used, n_blocks = _segment_tables(slot, n // TOKEN_TILE)
    xs = _dispatch(table, slot, h2, n_blocks * EXPERT_BLOCK)
    ys = _experts(xs, block, expert, valid, n_used, n_blocks, wgu, wd)
    return _combine(table, ys, gate, slot, h2, x1, mod3, wsgu, wsd, final_g, seq)


def _pad_cols(w, width):
    return jnp.pad(w, ((0, 0), (0, width - w.shape[1])))


def _rope_tables(seq):
    quarter = TINY_DIM // 4
    t = jnp.arange(seq)
    pos = jnp.stack([t // GRID_W, t % GRID_W], -1).astype(F32)
    inv_freq = ROPE_BASE ** (-jnp.arange(quarter, dtype=F32) / quarter)
    ang = pos[:, :, None] * inv_freq
    cos, sin = jnp.cos(ang), jnp.sin(ang)
    cos_t = jnp.stack([cos, cos], 2).reshape(seq, TINY_DIM)
    sin_t = jnp.stack([-sin, sin], 2).reshape(seq, TINY_DIM)
    return _pad_cols(cos_t, LANES), _pad_cols(sin_t, LANES)


def _half_swap_perm():
    quarter = TINY_DIM // 4
    idx = np.arange(TINY_DIM).reshape(2, 2, quarter)
    return idx[:, ::-1, :].reshape(TINY_DIM)


def _layer(x, c, ctx, c_ctx, w_ada, b_ada, norm_mix_g, norm_ffn_g, w_in, b_in,
           sgu_ln_g, sgu_ln_b, sgu_w_s, sgu_b_s, tiny_w_o, na_rpb, w_up_a, w_up_b, w_out,
           w_router, b_router, w_e_gate, w_e_up, w_e_down, w_sh_gate, w_sh_up, w_sh_down,
           final_g):
    b, s, d = x.shape
    n = b * s
    row = lambda v: v.reshape(1, -1)

    n_mod = -(-(b + 1) // SUBLANES) * SUBLANES
    c_all = jnp.concatenate([c, c_ctx[None], jnp.zeros((n_mod - b - 1, d), F32)], 0)
    mod = _ada(c_all, w_ada, b_ada)
    mod3 = mod[:b].reshape(b, 6, d)
    mod_ctx = mod[b, :2 * d].reshape(2, d)

    perm = _half_swap_perm()
    wcols = lambda lo, hi: w_in[:, lo:hi]
    bcols = lambda lo, hi: b_in[lo:hi]
    wq, bq = wcols(COL_TQ, COL_TK), bcols(COL_TQ, COL_TK)
    wk, bk = wcols(COL_TK, COL_TV), bcols(COL_TK, COL_TV)
    wv, bv = wcols(COL_TV, COL_NQ), bcols(COL_TV, COL_NQ)
    padw = lambda w: _pad_cols(w, LANES)
    padb = lambda v: jnp.pad(v, (0, LANES - v.shape[0]))
    wt = jnp.concatenate([padw(wq), padw(wq[:, perm]), padw(wk), padw(wk[:, perm]), padw(wv)],
                         1).astype(BF16)
    bt = row(jnp.concatenate([padb(bq), padb(bq[perm]), padb(bk), padb(bk[perm]), padb(bv)]))
    wt_ctx = jnp.concatenate([padw(wk), padw(wv)], 1).astype(BF16)
    bt_ctx = row(jnp.concatenate([padb(bk), padb(bv)]))
    wuv, buv = wcols(COL_U, COL_TQ).astype(BF16), row(bcols(COL_U, COL_TQ))
    wna, bna = wcols(COL_NQ, COL_GA).astype(BF16), row(bcols(COL_NQ, COL_GA))
    wna_ctx, bna_ctx = wcols(COL_NK, COL_GA).astype(BF16), row(bcols(COL_NK, COL_GA))
    wg, bg = wcols(COL_GA, IN_WIDTH).astype(BF16), row(bcols(COL_GA, IN_WIDTH))
    cos, sin = _rope_tables(s)

    x2 = x.reshape(n, d)
    u, vn, tq, tk, tv, nq, nk, nv, ga, gb = _inproj(
        x2, mod3, row(norm_mix_g), wuv, buv, row(sgu_ln_g), row(sgu_ln_b), wt, bt, cos, sin,
        wna, bna, wg, bg, s)
    tkc, tvc, nkc, nvc = _ctxproj(ctx.reshape(b * CTX_LEN, d), mod_ctx, row(norm_mix_g),
                                  wt_ctx, bt_ctx, wna_ctx, bna_ctx)

    per_b = lambda a, rows: a.reshape(b, rows, a.shape[-1])
    two = jnp.pad(tiny_w_o, ((0, LANES - TINY_DIM), (0, 0))).astype(BF16)
    oa = _mixer_a(per_b(tq, s), per_b(tk, s), per_b(tv, s), per_b(tkc, CTX_LEN),
                  per_b(tvc, CTX_LEN), per_b(vn, s), per_b(u, s),
                  sgu_w_s.astype(BF16), sgu_b_s.T, two)
    ob = _na(per_b(nq, s), per_b(nk, s), per_b(nv, s), per_b(nkc, CTX_LEN),
             per_b(nvc, CTX_LEN), _na_bias_table(na_rpb))

    wr = w_router.T.astype(BF16)
    br = b_router.reshape(N_EXPERTS, 1)
    x1, h2, gate, slot = _merge(oa.reshape(n, SGU_WIDTH), ob.reshape(n, NA_WIDTH), ga, gb, x2,
                                mod3, w_up_a.astype(BF16), w_up_b.astype(BF16),
                                w_out.astype(BF16), row(norm_ffn_g), wr, br, s)

    wgu = jnp.concatenate([w_e_gate, w_e_up], -1).astype(BF16)
    wd = w_e_down.astype(BF16)
    wsgu = jnp.concatenate([w_sh_gate, w_sh_up], -1).astype(BF16)
    wsd = w_sh_down.astype(BF16)
    out = _moe(h2, gate, slot, x1, mod3, wgu, wd, wsgu, wsd, row(final_g), s)
    return out.reshape(b, s, d)


def kernel(x, c, ctx, c_ctx, w_ada, b_ada, norm_mix_g, norm_ffn_g, w_in, b_in, sgu_ln_g, sgu_ln_b, sgu_w_s, sgu_b_s, tiny_w_o, na_rpb, w_up_a, w_up_b, w_out, w_router, b_router, w_e_gate, w_e_up, w_e_down, w_sh_gate, w_sh_up, w_sh_down, final_norm_g):
    assert w_ada.shape[0] == 1, "single-layer block"
    return _layer(x, c, ctx, c_ctx, w_ada[0], b_ada[0], norm_mix_g[0], norm_ffn_g[0], w_in[0],
                  b_in[0], sgu_ln_g[0], sgu_ln_b[0], sgu_w_s[0], sgu_b_s[0], tiny_w_o[0],
                  na_rpb[0], w_up_a[0], w_up_b[0], w_out[0], w_router[0], b_router[0],
                  w_e_gate[0], w_e_up[0], w_e_down[0], w_sh_gate[0], w_sh_up[0], w_sh_down[0],
                  final_norm_g)
```

```python
import functools

import numpy as np
import jax
import jax.numpy as jnp
from jax import lax
from jax.experimental import pallas as pl
from jax.experimental.pallas import tpu as pltpu

F32 = jnp.float32
BF16 = jnp.bfloat16

D_MODEL = 1024
CTX_LEN = 256
GRID_W = 64
NORM_EPS = 1e-6
SGU_WIDTH = 1024
SGU_GROUPS = 8
SGU_GROUP_DIM = SGU_WIDTH // SGU_GROUPS
CHUNK = 128
TINY_DIM = 64
ROPE_BASE = 10000.0
NA_HEADS = 16
NA_HEAD_DIM = 32
NA_WIDTH = NA_HEADS * NA_HEAD_DIM
NA_ROWS = 8
NA_COLS = 16
N_EXPERTS = 64
TOP_K = 8
EXPERT_FF = 256
SHARED_FF = 256
ROUTE_SCALE = 2.5
COL_U = 0
COL_V = COL_U + SGU_WIDTH
COL_TQ = COL_V + SGU_WIDTH
COL_TK = COL_TQ + TINY_DIM
COL_TV = COL_TK + TINY_DIM
COL_NQ = COL_TV + TINY_DIM
COL_NK = COL_NQ + NA_WIDTH
COL_NV = COL_NK + NA_WIDTH
COL_GA = COL_NV + NA_WIDTH
COL_GB = COL_GA + D_MODEL
IN_WIDTH = COL_GB + D_MODEL

LANES = 128
SUBLANES = 8
VMEM_LIMIT = 56 * 1024 * 1024

TOKEN_TILE = 256
MIX_CHUNKS = 4
NA_GROUP = LANES // NA_HEAD_DIM
NA_STEP_ROWS = 2
LOG2_E = float(np.log2(np.e))


def _dot(a, b):
    return jnp.dot(a, b, preferred_element_type=F32)


def _dot_nt(a, b):
    return lax.dot_general(a, b, (((1,), (1,)), ((), ())), preferred_element_type=F32)


def _dot_tn(a, b):
    return lax.dot_general(a, b, (((0,), (0,)), ((), ())), preferred_element_type=F32)


def _rms(x, g):
    return x * lax.rsqrt(jnp.mean(x * x, -1, keepdims=True) + NORM_EPS) * g


def _silu(x):
    return x * jax.nn.sigmoid(x)


def _gelu(x):
    return x * (lax.erf(x * np.float32(1.0 / np.sqrt(2.0))) + 1.0) * 0.5


def _params(*sem):
    return pltpu.CompilerParams(dimension_semantics=sem, vmem_limit_bytes=VMEM_LIMIT)


def _const_spec(shape):
    zeros = (0,) * len(shape)
    return pl.BlockSpec(shape, lambda *_: zeros)


def _ada_body(c_ref, w_ref, b_ref, o_ref):
    s = _silu(c_ref[...]).astype(BF16)
    o_ref[...] = _dot(s, w_ref[...].astype(BF16)) + b_ref[...]


def _ada(c_all, w_ada, b_ada):
    rows, d = c_all.shape
    width = w_ada.shape[1]
    return pl.pallas_call(
        _ada_body,
        grid=(width // d,),
        in_specs=[pl.BlockSpec((rows, d), lambda j: (0, 0)),
                  pl.BlockSpec((d, d), lambda j: (0, j)),
                  pl.BlockSpec((1, d), lambda j: (0, j))],
        out_specs=pl.BlockSpec((rows, d), lambda j: (0, j)),
        out_shape=jax.ShapeDtypeStruct((rows, width), F32),
        compiler_params=_params("arbitrary"),
        name="ada_mod",
    )(c_all, w_ada, b_ada.reshape(1, width))


def _inproj_body(x_ref, mod_ref, g_ref, wuv_ref, buv_ref, lng_ref, lnb_ref,
                 wt_ref, bt_ref, cos_ref, sin_ref, wna_ref, bna_ref, wg_ref, bg_ref,
                 u_ref, vn_ref, tq_ref, tk_ref, tv_ref, nq_ref, nk_ref, nv_ref,
                 ga_ref, gb_ref):
    y = _rms(x_ref[...], g_ref[...])
    h = (y * (1.0 + mod_ref[0, 1:2, :]) + mod_ref[0, 0:1, :]).astype(BF16)

    puv =_dot(h, wuv_ref[...]) + buv_ref[...]
    pt = _dot(h, wt_ref[...]) + bt_ref[...]
    pn = _dot(h, wna_ref[...]) + bna_ref[...]
    pg = _dot(h, wg_ref[...]) + bg_ref[...]

    u_ref[...] = _gelu(puv[:, :SGU_WIDTH]).astype(BF16)
    v = _gelu(puv[:, SGU_WIDTH:])
    mu = jnp.mean(v, -1, keepdims=True)
    vc = v - mu
    var = jnp.mean(vc * vc, -1, keepdims=True)
    vn_ref[...] = (vc * lax.rsqrt(var + NORM_EPS) * lng_ref[...] + lnb_ref[...]).astype(BF16)

    cos = cos_ref[...]
    sin = sin_ref[...]
    tq = pt[:, 0:LANES] * cos + pt[:, LANES:2 * LANES] * sin
    tq_ref[...] = (tq * np.float32(TINY_DIM ** -0.5)).astype(BF16)
    tk_ref[...] = (pt[:, 2 * LANES:3 * LANES] * cos + pt[:, 3 * LANES:4 * LANES] * sin).astype(BF16)
    tv_ref[...] = pt[:, 4 * LANES:5 * LANES].astype(BF16)

    nq_ref[...] = (pn[:, :NA_WIDTH] * np.float32(NA_HEAD_DIM ** -0.5 * LOG2_E)).astype(BF16)
    nk_ref[...] = pn[:, NA_WIDTH:2 * NA_WIDTH].astype(BF16)
    nv_ref[...] = pn[:, 2 * NA_WIDTH:].astype(BF16)

    ga_ref[...] = jax.nn.sigmoid(pg[:, :D_MODEL]).astype(BF16)
    gb_ref[...] = jax.nn.sigmoid(pg[:, D_MODEL:]).astype(BF16)


def _inproj(x2, mod3, g_mix, wuv, buv, lng, lnb, wt, bt, cos, sin, wna, bna, wg, bg, seq):
    n, d = x2.shape
    tm = TOKEN_TILE
    tpb = seq // tm
    tok = lambda w: pl.BlockSpec((tm, w), lambda i: (i, 0))
    pos = pl.BlockSpec((tm, LANES), lambda i: (i % tpb, 0))
    in_specs = [tok(d),
                pl.BlockSpec((1, 6, d), lambda i: (i // tpb, 0, 0)),
                _const_spec((1, d)),
                _const_spec(wuv.shape), _const_spec(buv.shape),
                _const_spec((1, d)), _const_spec((1, d)),
                _const_spec(wt.shape), _const_spec(bt.shape), pos, pos,
                _const_spec(wna.shape), _const_spec(bna.shape),
                _const_spec(wg.shape), _const_spec(bg.shape)]
    widths = [SGU_WIDTH, SGU_WIDTH, LANES, LANES, LANES, NA_WIDTH, NA_WIDTH, NA_WIDTH,
              D_MODEL, D_MODEL]
    return pl.pallas_call(
        _inproj_body,
        grid=(n // tm,),
        in_specs=in_specs,
        out_specs=[tok(w) for w in widths],
        out_shape=[jax.ShapeDtypeStruct((n, w), BF16) for w in widths],
        compiler_params=_params("arbitrary"),
        name="in_proj",
    )(x2, mod3, g_mix, wuv, buv, lng, lnb, wt, bt, cos, sin, wna, bna, wg, bg)


def _ctxproj_body(x_ref, mod_ref, g_ref, wt_ref, bt_ref, wna_ref, bna_ref,
                  tk_ref, tv_ref, nk_ref, nv_ref):
    y = _rms(x_ref[...], g_ref[...])
    h = (y * (1.0 + mod_ref[1:2, :]) + mod_ref[0:1, :]).astype(BF16)
    pt = _dot(h, wt_ref[...]) + bt_ref[...]
    tk_ref[...] = pt[:, :LANES].astype(BF16)
    tv_ref[...] = pt[:, LANES:].astype(BF16)
    pn = _dot(h, wna_ref[...]) + bna_ref[...]
    nk_ref[...] = pn[:, :NA_WIDTH].astype(BF16)
    nv_ref[...] = pn[:, NA_WIDTH:].astype(BF16)


def _ctxproj(ctx2, mod_ctx, g_mix, wt, bt, wna, bna):
    n, d = ctx2.shape
    tm = CTX_LEN
    tok = lambda w: pl.BlockSpec((tm, w), lambda i: (i, 0))
    widths = [LANES, LANES, NA_WIDTH, NA_WIDTH]
    return pl.pallas_call(
        _ctxproj_body,
        grid=(n // tm,),
        in_specs=[tok(d), _const_spec((2, d)), _const_spec((1, d)),
                  _const_spec(wt.shape), _const_spec(bt.shape),
                  _const_spec(wna.shape), _const_spec(bna.shape)],
        out_specs=[tok(w) for w in widths],
        out_shape=[jax.ShapeDtypeStruct((n, w), BF16) for w in widths],
        compiler_params=_params("arbitrary"),
        name="ctx_proj",
    )(ctx2, mod_ctx, g_mix, wt, bt, wna, bna)


def _mixer_a_body(tq_ref, tk_ref, tv_ref, tkc_ref, tvc_ref, vn_ref, u_ref,
                  ws_ref, bst_ref, two_ref, oa_ref):
    q = tq_ref[0]
    s_lat = _dot_nt(tk_ref[0], q)
    s_ctx = _dot_nt(tkc_ref[0], q)
    group = lambda g: slice(g * SGU_GROUP_DIM, (g + 1) * SGU_GROUP_DIM)
    chunk = lambda c: slice(c * CHUNK, (c + 1) * CHUNK)
    mixed = [_dot(ws_ref[g], jnp.concatenate([vn_ref[0, chunk(c), group(g)]
                                              for c in range(MIX_CHUNKS)], 1))
             for g in range(SGU_GROUPS)]
    m = jnp.maximum(jnp.max(s_lat, 0, keepdims=True), jnp.max(s_ctx, 0, keepdims=True))
    p_lat = jnp.exp(s_lat - m)
    p_ctx = jnp.exp(s_ctx - m)
    denom = jnp.sum(p_lat, 0, keepdims=True) + jnp.sum(p_ctx, 0, keepdims=True)
    att_t = _dot_tn(tv_ref[0], p_lat.astype(BF16)) + _dot_tn(tvc_ref[0], p_ctx.astype(BF16))
    att = (att_t / denom).T.astype(BF16)
    tiny = _dot(att, two_ref[...])

    for g in range(SGU_GROUPS):
        for c in range(MIX_CHUNKS):
            gate = (mixed[g][:, chunk(c)] + bst_ref[:, g:g + 1] + tiny[chunk(c), group(g)])
            oa_ref[0, chunk(c), group(g)] = (u_ref[0, chunk(c), group(g)].astype(F32)
                                             * gate).astype(BF16)


def _mixer_a(tq, tk, tv, tkc, tvc, vn, u, ws, bst, two):
    b, s, _ = tq.shape
    tile = MIX_CHUNKS * CHUNK
    blk = lambda w: pl.BlockSpec((1, tile, w), lambda i, j: (i, j, 0))
    per_b = lambda rows, w: pl.BlockSpec((1, rows, w), lambda i, j: (i, 0, 0))
    return pl.pallas_call(
        _mixer_a_body,
        grid=(b, s // tile),
        in_specs=[blk(LANES), per_b(s, LANES), per_b(s, LANES),
                  per_b(CTX_LEN, LANES), per_b(CTX_LEN, LANES),
                  blk(SGU_WIDTH), blk(SGU_WIDTH),
                  _const_spec(ws.shape), _const_spec(bst.shape), _const_spec(two.shape)],
        out_specs=blk(SGU_WIDTH),
        out_shape=jax.ShapeDtypeStruct((b, s, SGU_WIDTH), BF16),
        compiler_params=_params("arbitrary", "arbitrary"),
        name="mixer_a",
    )(tq, tk, tv, tkc, tvc, vn, u, ws, bst, two)


def _na_body(q_ref, k_ref, v_ref, kc_ref, vc_ref, *rest, rows):
    bias_refs, o_ref = rest[:-1], rest[-1]
    for i, bias_ref in enumerate(bias_refs):
        _na_row(pl.program_id(1) * len(bias_refs) + i, slice(i * GRID_W, (i + 1) * GRID_W),
                q_ref, k_ref, v_ref, kc_ref, vc_ref, bias_ref, o_ref, rows)


def _na_row(r, qrows, q_ref, k_ref, v_ref, kc_ref, vc_ref, bias_ref, o_ref, rows):
    r_start = jnp.clip(r - NA_ROWS // 2, 0, rows - NA_ROWS)
    k0 = pl.multiple_of(r_start * GRID_W, GRID_W)
    n_lat = NA_ROWS * GRID_W
    stacked = (NA_GROUP * GRID_W, LANES)
    own_lanes = (lax.broadcasted_iota(jnp.int32, stacked, 0) // GRID_W
                 == lax.broadcasted_iota(jnp.int32, stacked, 1) // NA_HEAD_DIM)
    n_groups = NA_WIDTH // LANES

    def scores(g):
        cols = slice(g * LANES, (g + 1) * LANES)
        qg = q_ref[0, qrows, cols]
        qm = jnp.concatenate([qg] * NA_GROUP, 0)
        qm = jnp.where(own_lanes, qm, jnp.zeros_like(qm))
        return (_dot_nt(k_ref[0, pl.ds(k0, n_lat), cols], qm) + bias_ref[0, g],
                _dot_nt(kc_ref[0, :, cols], qm))

    all_scores = [scores(g) for g in range(n_groups)]
    for g in range(n_groups):
        cols = slice(g * LANES, (g + 1) * LANES)
        s_lat, s_ctx = all_scores[g]
        m = jnp.maximum(jnp.max(s_lat, 0, keepdims=True), jnp.max(s_ctx, 0, keepdims=True))
        p_lat = jnp.exp2(s_lat - m)
        p_ctx = jnp.exp2(s_ctx - m)
        denom = jnp.sum(p_lat, 0, keepdims=True) + jnp.sum(p_ctx, 0, keepdims=True)
        o_t = (_dot_tn(v_ref[0, pl.ds(k0, n_lat), cols], p_lat.astype(BF16))
               + _dot_tn(vc_ref[0, :, cols], p_ctx.astype(BF16)))
        o = jnp.where(own_lanes, (o_t / denom).T, 0.0)
        acc = o[0:GRID_W]
        for j in range(1, NA_GROUP):
            acc = acc + o[j * GRID_W:(j + 1) * GRID_W]
        o_ref[0, qrows, cols] = acc.astype(BF16)


def _na(nq, nk, nv, nkc, nvc, bias):
    b, s, w = nq.shape
    rows = s // GRID_W
    kh = NA_ROWS
    per_step = NA_STEP_ROWS

    def bias_spec(i):
        def idx(bi, step):
            r = step * per_step + i
            return (r - jnp.clip(r - kh // 2, 0, rows - kh), 0, 0, 0)
        return pl.BlockSpec((1, NA_HEADS // NA_GROUP, kh * GRID_W, NA_GROUP * GRID_W), idx)

    per_b = lambda n: pl.BlockSpec((1, n, w), lambda i, r: (i, 0, 0))
    row = pl.BlockSpec((1, per_step * GRID_W, w), lambda i, r: (i, r, 0))
    return pl.pallas_call(
        functools.partial(_na_body, rows=rows),
        grid=(b, rows // per_step),
        in_specs=[row, per_b(s), per_b(s), per_b(CTX_LEN), per_b(CTX_LEN)]
        + [bias_spec(i) for i in range(per_step)],
        out_specs=row,
        out_shape=jax.ShapeDtypeStruct((b, s, w), BF16),
        compiler_params=_params("arbitrary", "arbitrary"),
        name="na_attn",
    )(nq, nk, nv, nkc, nvc, *([bias] * per_step))


def _na_bias_table(rpb):
    cq = np.arange(GRID_W)
    c_start = np.clip(cq - NA_COLS // 2, 0, GRID_W - NA_COLS)
    col_mask = (cq[None, :] >= c_start[:, None]) & (cq[None, :] < c_start[:, None] + NA_COLS)
    dc_idx = np.clip(cq[None, :] - cq[:, None], -(NA_COLS - 1), NA_COLS - 1) + NA_COLS - 1
    col_bias = rpb[:, :, dc_idx]
    col_bias = jnp.where(col_mask[None, None], col_bias, -jnp.inf)
    off = np.arange(NA_ROWS)[:, None]
    dr = np.arange(NA_ROWS)[None, :] - off + NA_ROWS - 1
    tab = col_bias[:, dr]
    tab = tab.transpose(1, 2, 4, 0, 3)
    tab = tab.reshape(NA_ROWS, NA_ROWS * GRID_W, NA_HEADS // NA_GROUP, NA_GROUP * GRID_W)
    return (tab.transpose(0, 2, 1, 3) * np.float32(LOG2_E)).astype(F32)


def _merge_body(oa_ref, ob_ref, ga_ref, gb_ref, x_ref, mod_ref, wua_ref, wub_ref, wo_ref,
                gf_ref, wr_ref, br_ref, x1_ref, h2_ref, gate_ref, slot_ref):
    a = _dot(oa_ref[...], wua_ref[...])
    b = _dot(ob_ref[...], wub_ref[...])
    merged = ga_ref[...].astype(F32) * a + gb_ref[...].astype(F32) * b
    x1 = x_ref[...] + mod_ref[0, 2:3, :] * _dot(merged.astype(BF16), wo_ref[...])
    x1_ref[...] = x1
    h2 = _rms(x1, gf_ref[...]) * (1.0 + mod_ref[0, 4:5, :]) + mod_ref[0, 3:4, :]
    h2b = h2.astype(BF16)
    h2_ref[...] = h2b

    tm = h2b.shape[0]
    scores = jax.nn.sigmoid(_dot_nt(wr_ref[...], h2b))
    expert = lax.broadcasted_iota(jnp.int32, scores.shape, 0)
    sel = scores + br_ref[...]
    picked = jnp.zeros_like(scores)
    for _ in range(TOP_K):
        best = jnp.max(sel, 0, keepdims=True)
        first = jnp.min(jnp.where(sel == best, expert, N_EXPERTS), 0, keepdims=True)
        hit = expert == first
        picked = jnp.where(hit, scores, picked)
        sel = jnp.where(hit, -jnp.inf, sel)
    gate = picked / jnp.sum(picked, 0, keepdims=True) * np.float32(ROUTE_SCALE)

    routed = picked > 0.0
    earlier = (lax.broadcasted_iota(jnp.int32, (tm, tm), 0)
               < lax.broadcasted_iota(jnp.int32, (tm, tm), 1))
    slot = _dot(jnp.where(routed, 1.0, 0.0).astype(BF16), earlier.astype(BF16))
    slot = jnp.where(routed, slot, -1.0)
    pad = LANES - N_EXPERTS
    gate_ref[...] = jnp.concatenate([gate, jnp.zeros((pad, tm), F32)], 0).T
    slot_ref[...] = jnp.concatenate([slot, jnp.full((pad, tm), -1.0, F32)], 0).T


def _merge(oa, ob, ga, gb, x2, mod3, wua, wub, wo, g_ffn_norm, wr, br, seq):
    n, d = x2.shape
    tm = TOKEN_TILE
    tpb = seq // tm
    tok = lambda w: pl.BlockSpec((tm, w), lambda i: (i, 0))
    return pl.pallas_call(
        _merge_body,
        grid=(n // tm,),
        in_specs=[tok(SGU_WIDTH), tok(NA_WIDTH), tok(d), tok(d), tok(d),
                  pl.BlockSpec((1, 6, d), lambda i: (i // tpb, 0, 0)),
                  _const_spec(wua.shape), _const_spec(wub.shape), _const_spec(wo.shape),
                  _const_spec((1, d)), _const_spec(wr.shape), _const_spec(br.shape)],
        out_specs=[tok(d), tok(d), tok(LANES), tok(LANES)],
        out_shape=[jax.ShapeDtypeStruct((n, d), F32),
                   jax.ShapeDtypeStruct((n, d), BF16),
                   jax.ShapeDtypeStruct((n, LANES), F32),
                   jax.ShapeDtypeStruct((n, LANES), F32)],
        compiler_params=_params("arbitrary"),
        name="merge_route",
    )(oa, ob, ga, gb, x2, mod3, wua, wub, wo, g_ffn_norm, wr, br)


SLAB = 16
EXPERT_BLOCK = 1024
LOCAL_ROWS = 3072
ROW_CHUNK = 1024
assert LOCAL_ROWS >= TOKEN_TILE * TOP_K + N_EXPERTS * (SLAB - 1) and LOCAL_ROWS % ROW_CHUNK == 0
LOCAL_SLABS = LOCAL_ROWS // SLAB
WAIT_SLABS = 8
RUNS = (4, 2, 1)
RUN_CAPS = (LOCAL_SLABS // 4, N_EXPERTS, N_EXPERTS)
TAB_SLABS = len(RUNS)
TAB_LISTS = SUBLANES
TAB_WIDTH = 3 * LANES
assert TAB_LISTS + 2 * sum(RUN_CAPS) <= TAB_WIDTH


def _segment_tables(slot, n_tiles):
    routed = (slot[:, :N_EXPERTS] >= 0).reshape(n_tiles, TOKEN_TILE, N_EXPERTS)
    cnt = jnp.sum(routed.astype(jnp.int32), 1)
    size = (cnt + SLAB - 1) // SLAB
    end = jnp.cumsum(size, 1)
    off = end - size
    total = jnp.sum(size, 0)
    per_block = EXPERT_BLOCK // SLAB
    blocks = (total + per_block - 1) // per_block
    block_end = jnp.cumsum(blocks)
    block_start = block_end - blocks
    base = block_start[None, :] * per_block + jnp.cumsum(size, 0) - size

    counts, lists = [], []
    done = jnp.zeros_like(size)
    for run, cap in zip(RUNS, RUN_CAPS):
        n = (size - done) // run
        n_end = jnp.cumsum(n, 1)
        n_start = n_end - n
        i = jnp.arange(cap)[None, :, None]
        owned = (n_start[:, None, :] <= i) & (i < n_end[:, None, :])
        k = run * (i - n_start[:, None, :])
        for first in (off, base):
            lists.append(jnp.sum(jnp.where(owned, (first + done)[:, None, :] + k, 0), -1))
        counts.append(n_end[:, -1:])
        done = done + n * run
    head = jnp.concatenate(counts + [end[:, -1:]], 1)
    head = jnp.pad(head, ((0, 0), (0, TAB_LISTS - head.shape[1])))
    table = jnp.concatenate([head] + lists, 1)
    table = jnp.pad(table, ((0, 0), (0, TAB_WIDTH - table.shape[1])))
    table = table.astype(jnp.int32)[:, None, :]

    n_blocks = -(-(n_tiles * LOCAL_SLABS) // per_block) + N_EXPERTS
    n_used = block_end[-1]
    b = jnp.minimum(jnp.arange(n_blocks), n_used - 1)
    expert = jnp.minimum(jnp.sum((block_end[None, :] <= b[:, None]).astype(jnp.int32), -1),
                         N_EXPERTS - 1)
    valid = jnp.clip(total[expert] * SLAB - (b - block_start[expert]) * EXPERT_BLOCK,
                     0, EXPERT_BLOCK)
    return (table, b.astype(jnp.int32), expert.astype(jnp.int32), valid.astype(jnp.int32),
            n_used.astype(jnp.int32).reshape(1), n_blocks)


def _for_each_copy(tab_ref, fn):
    start = TAB_LISTS
    for r, (run, cap) in enumerate(zip(RUNS, RUN_CAPS)):
        def one(i, c, start=start, run=run, cap=cap):
            local = tab_ref[0, 0, start + i]
            glob = tab_ref[0, 0, start + cap + i]
            fn(pl.multiple_of(local * SLAB, SLAB), pl.multiple_of(glob * SLAB, SLAB), run * SLAB)
            return c
        lax.fori_loop(0, tab_ref[0, 0, r], one, 0)
        start += 2 * cap


def _drain_slabs(n, wait_many, wait_one):
    shift = WAIT_SLABS.bit_length() - 1
    lax.fori_loop(0, lax.shift_right_logical(n, shift), lambda q, c: (wait_many(), c)[1], 0)
    lax.fori_loop(0, n & (WAIT_SLABS - 1), lambda q, c: (wait_one(), c)[1], 0)


def _slab_sizes(count):
    return jnp.floor((count + (SLAB - 1.0)) * (1.0 / SLAB)) * SLAB


def _dispatch_body(tab_ref, slot_ref, h_ref, xs_ref, xbuf_ref, sem_ref, pending_ref):
    j = pl.program_id(0)
    cur = j % 2
    slot = slot_ref[...]
    count = jnp.sum((slot >= 0.0).astype(F32), 0, keepdims=True)
    size = _slab_sizes(count)
    earlier = (lax.broadcasted_iota(jnp.int32, (LANES, LANES), 0)
               < lax.broadcasted_iota(jnp.int32, (LANES, LANES), 1))
    off = _dot(jnp.broadcast_to(size, (SUBLANES, LANES)).astype(BF16),
               earlier.astype(BF16))[0:1]
    slot_t = slot.T.astype(BF16)
    h = h_ref[...]
    for c in range(LOCAL_ROWS // ROW_CHUNK):
        r = (lax.broadcasted_iota(jnp.int32, (ROW_CHUNK, LANES), 0) + c * ROW_CHUNK).astype(F32)
        seg = (r >= off) & (r < off + size)
        pos = jnp.sum(jnp.where(seg, r - off + 1.0, 0.0), 1, keepdims=True) - 1.0
        want = _dot(jnp.where(seg, 1.0, 0.0).astype(BF16), slot_t)
        onehot = jnp.where(want == pos, 1.0, 0.0).astype(BF16)
        xbuf_ref[cur, c * ROW_CHUNK:(c + 1) * ROW_CHUNK, :] = _dot(onehot, h).astype(BF16)

    def slab_copy(buf, local_row, global_row, rows=SLAB):
        return pltpu.make_async_copy(xbuf_ref.at[buf, pl.ds(local_row, rows), :],
                                     xs_ref.at[pl.ds(global_row, rows), :], sem_ref.at[buf])

    _for_each_copy(tab_ref, lambda lr, gr, rows: slab_copy(cur, lr, gr, rows).start())

    def drain(buf, n):
        _drain_slabs(n, lambda: slab_copy(buf, 0, 0, WAIT_SLABS * SLAB).wait(),
                     lambda: slab_copy(buf, 0, 0).wait())

    @pl.when(j > 0)
    def _():
        drain(1 - cur, pending_ref[0])

    pending_ref[0] = tab_ref[0, 0, TAB_SLABS]

    @pl.when(j == pl.num_programs(0) - 1)
    def _():
        drain(cur, pending_ref[0])


def _dispatch(table, slot, h2, n_rows):
    n, d = h2.shape
    tm = TOKEN_TILE
    return pl.pallas_call(
        _dispatch_body,
        grid=(n // tm,),
        in_specs=[pl.BlockSpec((1, 1, TAB_WIDTH), lambda i: (i, 0, 0),
                               memory_space=pltpu.SMEM),
                  pl.BlockSpec((tm, LANES), lambda i: (i, 0)),
                  pl.BlockSpec((tm, d), lambda i: (i, 0))],
        out_specs=pl.BlockSpec(memory_space=pl.ANY),
        out_shape=jax.ShapeDtypeStruct((n_rows, d), BF16),
        scratch_shapes=[pltpu.VMEM((2, LOCAL_ROWS, d), BF16),
                        pltpu.SemaphoreType.DMA((2,)),
                        pltpu.SMEM((1,), jnp.int32)],
        compiler_params=_params("arbitrary"),
        name="moe_dispatch",
    )(table, slot, h2)


def _expert_body(blk_ref, exp_ref, valid_ref, used_ref, x_ref, wgu_ref, wd_ref, y_ref):
    b = pl.program_id(0)

    @pl.when(b < used_ref[0])
    def _():
        row = lax.broadcasted_iota(jnp.int32, x_ref.shape, 0)
        x = jnp.where(row < valid_ref[b], x_ref[...], jnp.zeros_like(x_ref))
        a = _dot(x, wgu_ref[0])
        act = _silu(a[:, :EXPERT_FF]) * a[:, EXPERT_FF:]
        y_ref[...] = _dot(act.astype(BF16), wd_ref[0]).astype(BF16)


def _experts(xs, block, expert, valid, n_used, n_blocks, wgu, wd):
    _, d = xs.shape
    rows = pl.BlockSpec((EXPERT_BLOCK, d), lambda b, blk, exp, val, used: (blk[b], 0))
    grid_spec = pltpu.PrefetchScalarGridSpec(
        num_scalar_prefetch=4,
        grid=(n_blocks,),
        in_specs=[rows,
                  pl.BlockSpec((1, d, 2 * EXPERT_FF), lambda b, blk, exp, val, used: (exp[b], 0, 0)),
                  pl.BlockSpec((1, EXPERT_FF, d), lambda b, blk, exp, val, used: (exp[b], 0, 0))],
        out_specs=rows)
    return pl.pallas_call(
        _expert_body,
        grid_spec=grid_spec,
        out_shape=jax.ShapeDtypeStruct(xs.shape, BF16),
        compiler_params=_params("arbitrary"),
        name="moe_experts",
    )(block, expert, valid, n_used, xs, wgu, wd)


def _combine_body(tab_ref, nxt_ref, gate_ref, slot_ref, h_ref, x1_ref, mod_ref, wsgu_ref,
                  wsd_ref, fg_ref, ys_ref, o_ref, pw_ref, ybuf_ref, sem_ref):
    j = pl.program_id(0)
    cur = j % 2

    def slab_copy(buf, local_row, global_row, rows=SLAB):
        return pltpu.make_async_copy(ys_ref.at[pl.ds(global_row, rows), :],
                                     ybuf_ref.at[buf, pl.ds(local_row, rows), :], sem_ref.at[buf])

    @pl.when(j == 0)
    def _():
        ybuf_ref[...] = jnp.zeros_like(ybuf_ref)
        _for_each_copy(tab_ref, lambda lr, gr, rows: slab_copy(0, lr, gr, rows).start())

    @pl.when(j < pl.num_programs(0) - 1)
    def _():
        _for_each_copy(nxt_ref, lambda lr, gr, rows: slab_copy(1 - cur, lr, gr, rows).start())

    slot = slot_ref[...]
    slot_t = slot.T
    count = jnp.sum((slot_t >= 0.0).astype(F32), 1, keepdims=True)
    size = _slab_sizes(count)
    earlier = (lax.broadcasted_iota(jnp.int32, (LANES, LANES), 1)
               < lax.broadcasted_iota(jnp.int32, (LANES, LANES), 0))
    off = _dot(earlier.astype(BF16),
               jnp.broadcast_to(size, (LANES, LANES)).astype(BF16))[:, 0:1]
    slot_b = slot.astype(BF16)
    gate_b = gate_ref[...].astype(BF16)
    for c in range(LOCAL_ROWS // ROW_CHUNK):
        r = (lax.broadcasted_iota(jnp.int32, (LANES, ROW_CHUNK), 1) + c * ROW_CHUNK).astype(F32)
        seg = (r >= off) & (r < off + size)
        pos = jnp.sum(jnp.where(seg, r - off + 1.0, 0.0), 0, keepdims=True) - 1.0
        seg_b = jnp.where(seg, 1.0, 0.0).astype(BF16)
        want = _dot(slot_b, seg_b)
        weight = _dot(gate_b, seg_b)
        pw_ref[:, c * ROW_CHUNK:(c + 1) * ROW_CHUNK] = jnp.where(want == pos, weight,
                                                                 0.0).astype(BF16)

    a = _dot(h_ref[...], wsgu_ref[...])
    act = _silu(a[:, :SHARED_FF]) * a[:, SHARED_FF:]
    shared = _dot(act.astype(BF16), wsd_ref[...])

    _drain_slabs(tab_ref[0, 0, TAB_SLABS],
                 lambda: slab_copy(cur, 0, 0, WAIT_SLABS * SLAB).wait(),
                 lambda: slab_copy(cur, 0, 0).wait())
    routed = _dot(pw_ref[...], ybuf_ref[cur])
    y = x1_ref[...] + mod_ref[0, 5:6, :] * (shared + routed)
    o_ref[...] = _rms(y, fg_ref[...])


def _combine(table, ys, gate, slot, h2, x1, mod3, wsgu, wsd, final_g, seq):
    n, d = h2.shape
    tm = TOKEN_TILE
    tpb = seq // tm
    last = n // tm - 1
    tok = lambda w: pl.BlockSpec((tm, w), lambda i: (i, 0))
    tab = lambda f: pl.BlockSpec((1, 1, TAB_WIDTH), lambda i: (f(i), 0, 0),
                                 memory_space=pltpu.SMEM)
    return pl.pallas_call(
        _combine_body,
        grid=(n // tm,),
        in_specs=[tab(lambda i: i), tab(lambda i: jnp.minimum(i + 1, last)),
                  tok(LANES), tok(LANES), tok(d), tok(d),
                  pl.BlockSpec((1, 6, d), lambda i: (i // tpb, 0, 0)),
                  _const_spec(wsgu.shape), _const_spec(wsd.shape), _const_spec((1, d)),
                  pl.BlockSpec(memory_space=pl.ANY)],
        out_specs=tok(d),
        out_shape=jax.ShapeDtypeStruct((n, d), F32),
        scratch_shapes=[pltpu.VMEM((tm, LOCAL_ROWS), BF16),
                        pltpu.VMEM((2, LOCAL_ROWS, d), BF16),
                        pltpu.SemaphoreType.DMA((2,))],
        compiler_params=_params("arbitrary"),
        name="moe_combine",
    )(table, table, gate, slot, h2, x1, mod3, wsgu, wsd, final_g, ys)


def _moe(h2, gate, slot, x1, mod3, wgu, wd, wsgu, wsd, final_g, seq):
    n, _ = h2.shape
    table, block, expert, valid, n_used, n_blocks = _segment_tables(slot, n // TOKEN_TILE)
    xs = _dispatch(table, slot, h2, n_blocks * EXPERT_BLOCK)
    ys = _experts(xs, block, expert, valid, n_used, n_blocks, wgu, wd)
    return _combine(table, ys, gate, slot, h2, x1, mod3, wsgu, wsd, final_g, seq)


def _pad_cols(w, width):
    return jnp.pad(w, ((0, 0), (0, width - w.shape[1])))


def _rope_tables(seq):
    quarter = TINY_DIM // 4
    t = jnp.arange(seq)
    pos = jnp.stack([t // GRID_W, t % GRID_W], -1).astype(F32)
    inv_freq = ROPE_BASE ** (-jnp.arange(quarter, dtype=F32) / quarter)
    ang = pos[:, :, None] * inv_freq
    cos, sin = jnp.cos(ang), jnp.sin(ang)
    cos_t = jnp.stack([cos, cos], 2).reshape(seq, TINY_DIM)
    sin_t = jnp.stack([-sin, sin], 2).reshape(seq, TINY_DIM)
    return _pad_cols(cos_t, LANES), _pad_cols(sin_t, LANES)


def _half_swap_perm():
    quarter = TINY_DIM // 4
    idx = np.arange(TINY_DIM).reshape(2, 2, quarter)
    return idx[:, ::-1, :].reshape(TINY_DIM)


def _layer(x, c, ctx, c_ctx, w_ada, b_ada, norm_mix_g, norm_ffn_g, w_in, b_in,
           sgu_ln_g, sgu_ln_b, sgu_w_s, sgu_b_s, tiny_w_o, na_rpb, w_up_a, w_up_b, w_out,
           w_router, b_router, w_e_gate, w_e_up, w_e_down, w_sh_gate, w_sh_up, w_sh_down,
           final_g):
    b, s, d = x.shape
    n = b * s
    row = lambda v: v.reshape(1, -1)

    n_mod = -(-(b + 1) // SUBLANES) * SUBLANES
    c_all = jnp.concatenate([c, c_ctx[None], jnp.zeros((n_mod - b - 1, d), F32)], 0)
    mod = _ada(c_all, w_ada, b_ada)
    mod3 = mod[:b].reshape(b, 6, d)
    mod_ctx = mod[b, :2 * d].reshape(2, d)

    perm = _half_swap_perm()
    wcols = lambda lo, hi: w_in[:, lo:hi]
    bcols = lambda lo, hi: b_in[lo:hi]
    wq, bq = wcols(COL_TQ, COL_TK), bcols(COL_TQ, COL_TK)
    wk, bk = wcols(COL_TK, COL_TV), bcols(COL_TK, COL_TV)
    wv, bv = wcols(COL_TV, COL_NQ), bcols(COL_TV, COL_NQ)
    padw = lambda w: _pad_cols(w, LANES)
    padb = lambda v: jnp.pad(v, (0, LANES - v.shape[0]))
    wt = jnp.concatenate([padw(wq), padw(wq[:, perm]), padw(wk), padw(wk[:, perm]), padw(wv)],
                         1).astype(BF16)
    bt = row(jnp.concatenate([padb(bq), padb(bq[perm]), padb(bk), padb(bk[perm]), padb(bv)]))
    wt_ctx = jnp.concatenate([padw(wk), padw(wv)], 1).astype(BF16)
    bt_ctx = row(jnp.concatenate([padb(bk), padb(bv)]))
    wuv, buv = wcols(COL_U, COL_TQ).astype(BF16), row(bcols(COL_U, COL_TQ))
    wna, bna = wcols(COL_NQ, COL_GA).astype(BF16), row(bcols(COL_NQ, COL_GA))
    wna_ctx, bna_ctx = wcols(COL_NK, COL_GA).astype(BF16), row(bcols(COL_NK, COL_GA))
    wg, bg = wcols(COL_GA, IN_WIDTH).astype(BF16), row(bcols(COL_GA, IN_WIDTH))
    cos, sin = _rope_tables(s)

    x2 = x.reshape(n, d)
    u, vn, tq, tk, tv, nq, nk, nv, ga, gb = _inproj(
        x2, mod3, row(norm_mix_g), wuv, buv, row(sgu_ln_g), row(sgu_ln_b), wt, bt, cos, sin,
        wna, bna, wg, bg, s)
    tkc, tvc, nkc, nvc = _ctxproj(ctx.reshape(b * CTX_LEN, d), mod_ctx, row(norm_mix_g),
                                  wt_ctx, bt_ctx, wna_ctx, bna_ctx)

    per_b = lambda a, rows: a.reshape(b, rows, a.shape[-1])
    two = jnp.pad(tiny_w_o, ((0, LANES - TINY_DIM), (0, 0))).astype(BF16)
    oa = _mixer_a(per_b(tq, s), per_b(tk, s), per_b(tv, s), per_b(tkc, CTX_LEN),
                  per_b(tvc, CTX_LEN), per_b(vn, s), per_b(u, s),
                  sgu_w_s.astype(BF16), sgu_b_s.T, two)
    ob = _na(per_b(nq, s), per_b(nk, s), per_b(nv, s), per_b(nkc, CTX_LEN),
             per_b(nvc, CTX_LEN), _na_bias_table(na_rpb))

    wr = w_router.T.astype(BF16)
    br = b_router.reshape(N_EXPERTS, 1)
    x1, h2, gate, slot = _merge(oa.reshape(n, SGU_WIDTH), ob.reshape(n, NA_WIDTH), ga, gb, x2,
                                mod3, w_up_a.astype(BF16), w_up_b.astype(BF16),
                                w_out.astype(BF16), row(norm_ffn_g), wr, br, s)

    wgu = jnp.concatenate([w_e_gate, w_e_up], -1).astype(BF16)
    wd = w_e_down.astype(BF16)
    wsgu = jnp.concatenate([w_sh_gate, w_sh_up], -1).astype(BF16)
    wsd = w_sh_down.astype(BF16)
    out = _moe(h2, gate, slot, x1, mod3, wgu, wd, wsgu, wsd, row(final_g), s)
    return out.reshape(b, s, d)


def kernel(x, c, ctx, c_ctx, w_ada, b_ada, norm_mix_g, norm_ffn_g, w_in, b_in, sgu_ln_g, sgu_ln_b, sgu_w_s, sgu_b_s, tiny_w_o, na_rpb, w_up_a, w_up_b, w_out, w_router, b_router, w_e_gate, w_e_up, w_e_down, w_sh_gate, w_sh_up, w_sh_down, final_norm_g):
    assert w_ada.shape[0] == 1, "single-layer block"
    return _layer(x, c, ctx, c_ctx, w_ada[0], b_ada[0], norm_mix_g[0], norm_ffn_g[0], w_in[0],
                  b_in[0], sgu_ln_g[0], sgu_ln_b[0], sgu_w_s[0], sgu_b_s[0], tiny_w_o[0],
                  na_rpb[0], w_up_a[0], w_up_b[0], w_out[0], w_router[0], b_router[0],
                  w_e_gate[0], w_e_up[0], w_e_down[0], w_sh_gate[0], w_sh_up[0], w_sh_down[0],
                  final_norm_g)
```
